```python
import jax, jax.numpy as jnp
from jax import lax
import numpy as np

D_MODEL = 1024
BATCH = 4
SEQ = 8192
DEPTH = 2

CHUNK = 64
N_MIXERS = 2
N_MEM = 256
XATTN_HEADS = 4
XATTN_HEAD_DIM = 64
XATTN_WIDTH = XATTN_HEADS * XATTN_HEAD_DIM
MIX_WIDTH = D_MODEL - XATTN_WIDTH
CONV_WIDTH = 3
POOL_WINDOWS = (2, 4, 8, 16)
POOL_GROUPS = len(POOL_WINDOWS)
POOL_GROUP_DIM = MIX_WIDTH // POOL_GROUPS
D_FF = 2816
RMS_EPS = 1e-6
N_LAYERS_A = (DEPTH + 1) // 2
N_LAYERS_B = DEPTH // 2

kernel_name = "hybrid_conv_pool_memxattn_macaron"


def rms_norm(x, g):
    xf = x.astype(jnp.float32)
    y = xf * lax.rsqrt(jnp.mean(xf * xf, axis=-1, keepdims=True) + RMS_EPS)
    return (y * g.astype(jnp.float32)).astype(x.dtype)


def swiglu(h, w_gu, w_down):
    g, u = jnp.split(h @ w_gu, 2, axis=-1)
    return (jax.nn.silu(g) * u) @ w_down


def causal_short_conv(u, w):
    s = u.shape[1]
    up = jnp.pad(u, ((0, 0), (CONV_WIDTH - 1, 0), (0, 0)))
    y = w[0] * up[:, 0:s]
    for k in range(1, CONV_WIDTH):
        y = y + w[k] * up[:, k:k + s]
    return y


def multiscale_pool(p, w_group, scale):
    s = p.shape[1]
    pos = jnp.arange(s, dtype=jnp.float32)[None, :, None]
    outs = []
    for gi, win in enumerate(POOL_WINDOWS):
        pg = p[..., gi * POOL_GROUP_DIM:(gi + 1) * POOL_GROUP_DIM]
        pf = pg.astype(jnp.float32)
        csum = jnp.cumsum(pf, axis=1)
        cfull = jnp.pad(csum, ((0, 0), (1, 0), (0, 0)))
        lower = jnp.pad(cfull[:, :s - win + 1], ((0, 0), (win - 1, 0), (0, 0)))
        count = jnp.minimum(pos + 1.0, float(win))
        mean = (csum - lower) / count
        outs.append(((mean - pf).astype(p.dtype)) @ w_group[gi])
    return jnp.concatenate(outs, axis=-1) * scale


def mem_cross_attn(q, mem_h, w_kv):
    b, s, _ = q.shape
    k, v = jnp.split(mem_h @ w_kv, 2, axis=-1)
    qh = q.reshape(b, s, XATTN_HEADS, XATTN_HEAD_DIM)
    kh = k.reshape(b, N_MEM, XATTN_HEADS, XATTN_HEAD_DIM)
    vh = v.reshape(b, N_MEM, XATTN_HEADS, XATTN_HEAD_DIM)
    scores = jnp.einsum('bshd,bmhd->bhsm', qh, kh).astype(jnp.float32) * (XATTN_HEAD_DIM ** -0.5)
    probs = jax.nn.softmax(scores, axis=-1).astype(q.dtype)
    out = jnp.einsum('bhsm,bmhd->bshd', probs, vh)
    return out.reshape(b, s, XATTN_WIDTH)


def setup_inputs(seed: int = 0) -> dict:
    key = jax.random.key(seed)
    ks = jax.random.split(key, 24)
    f32 = jnp.float32

    def nrm(k, shape, scale):
        return jax.random.normal(k, shape, f32) * scale

    def gain(k, shape):
        return 1.0 + 0.05 * jax.random.normal(k, shape, f32)

    return {
        "x": jax.random.normal(ks[0], (BATCH, SEQ, D_MODEL), f32),
        "mem": jax.random.normal(ks[1], (BATCH, N_MEM, D_MODEL), f32),
        "ffn1_norm": gain(ks[2], (DEPTH, D_MODEL)),
        "ffn1_w_gu": nrm(ks[3], (DEPTH, D_MODEL, 2 * D_FF), D_MODEL ** -0.5),
        "ffn1_w_down": nrm(ks[4], (DEPTH, D_FF, D_MODEL), D_FF ** -0.5),
        "mix_norm": gain(ks[5], (DEPTH, D_MODEL)),
        "mem_norm": gain(ks[6], (DEPTH, D_MODEL)),
        "w_kv": nrm(ks[7], (DEPTH, D_MODEL, 2 * XATTN_WIDTH), D_MODEL ** -0.5),
        "w_out": nrm(ks[8], (DEPTH, D_MODEL, D_MODEL), D_MODEL ** -0.5),
        "conv_w_in": nrm(ks[9], (N_LAYERS_A, D_MODEL, 3 * MIX_WIDTH + XATTN_WIDTH), D_MODEL ** -0.5),
        "conv_w": nrm(ks[10], (N_LAYERS_A, CONV_WIDTH, MIX_WIDTH), CONV_WIDTH ** -0.5),
        "pool_w_in": nrm(ks[11], (N_LAYERS_B, D_MODEL, MIX_WIDTH + XATTN_WIDTH), D_MODEL ** -0.5),
        "pool_w_group": nrm(ks[12], (N_LAYERS_B, POOL_GROUPS, POOL_GROUP_DIM, POOL_GROUP_DIM), POOL_GROUP_DIM ** -0.5),
        "pool_scale": 1.0 + 0.1 * jax.random.normal(ks[13], (N_LAYERS_B, MIX_WIDTH), f32),
        "ffn2_norm": gain(ks[14], (DEPTH, D_MODEL)),
        "ffn2_w_gu": nrm(ks[15], (DEPTH, D_MODEL, 2 * D_FF), D_MODEL ** -0.5),
        "ffn2_w_down": nrm(ks[16], (DEPTH, D_FF, D_MODEL), D_FF ** -0.5),
        "final_norm": gain(ks[17], (D_MODEL,)),
    }


def reference(x, mem, ffn1_norm, ffn1_w_gu, ffn1_w_down, mix_norm, mem_norm, w_kv, w_out,
              conv_w_in, conv_w, pool_w_in, pool_w_group, pool_scale,
              ffn2_norm, ffn2_w_gu, ffn2_w_down, final_norm):
    ia = 0
    ib = 0
    for i in range(DEPTH):
        x = x + 0.5 * swiglu(rms_norm(x, ffn1_norm[i]), ffn1_w_gu[i], ffn1_w_down[i])
        h = rms_norm(x, mix_norm[i])
        mem_h = rms_norm(mem, mem_norm[i])
        if i % N_MIXERS == 0:
            z = h @ conv_w_in[ia]
            gate_b = z[..., :MIX_WIDTH]
            gate_c = z[..., MIX_WIDTH:2 * MIX_WIDTH]
            val = z[..., 2 * MIX_WIDTH:3 * MIX_WIDTH]
            q = z[..., 3 * MIX_WIDTH:]
            mix = gate_b * causal_short_conv(gate_c * val, conv_w[ia])
            ia += 1
        else:
            z = h @ pool_w_in[ib]
            mix = multiscale_pool(z[..., :MIX_WIDTH], pool_w_group[ib], pool_scale[ib])
            q = z[..., MIX_WIDTH:]
            ib += 1
        att = mem_cross_attn(q, mem_h, w_kv[i])
        x = x + jnp.concatenate([mix, att], axis=-1) @ w_out[i]
        x = x + 0.5 * swiglu(rms_norm(x, ffn2_norm[i]), ffn2_w_gu[i], ffn2_w_down[i])
    return rms_norm(x, final_norm)
```

```python
import functools

import jax
import jax.numpy as jnp
from jax import lax
from jax.experimental import pallas as pl
from jax.experimental.pallas import tpu as pltpu

D_MODEL = 1024
N_MEM = 256
HEADS = 4
HEAD_DIM = 64
XW = HEADS * HEAD_DIM
MW = D_MODEL - XW
CONV_WIDTH = 3
POOL_WINDOWS = (2, 4, 8, 16)
POOL_GROUP_DIM = MW // len(POOL_WINDOWS)
D_FF = 2816
RMS_EPS = 1e-6

ROW_TILE = 512
CARRY_ROWS = 16
FF_CHUNKS = ((0, 1024), (1024, 1024), (2048, 768))
VMEM_LIMIT_BYTES = 56 * 1024 * 1024

_BF16 = jnp.bfloat16
_F32 = jnp.float32


def _rms_norm(x, g):
    y = x * lax.rsqrt(jnp.mean(x * x, axis=-1, keepdims=True) + RMS_EPS)
    return y * g


def _dot(a, b):
    return jnp.dot(a, b, preferred_element_type=_F32)


def _resident(shape):
    nd = len(shape)
    return pl.BlockSpec(shape, lambda *_: (0,) * nd, pipeline_mode=pl.Buffered(1))


def _kv_kernel(mem_ref, g_ref, wkv_ref, kbd_ref, vobd_ref):
    mem_h = _rms_norm(mem_ref[0], g_ref[0]).astype(_BF16)
    kv = _dot(mem_h, wkv_ref[0])
    k = kv[:, :XW] * (HEAD_DIM ** -0.5)
    v = kv[:, XW:]
    kt = jnp.concatenate([k.T] * HEADS, axis=1)
    r = lax.broadcasted_iota(jnp.int32, kt.shape, 0) // HEAD_DIM
    c = lax.broadcasted_iota(jnp.int32, kt.shape, 1) // N_MEM
    kbd_ref[0, 0] = jnp.where(r == c, kt, 0.0).astype(_BF16)
    vt = jnp.concatenate([v] * HEADS, axis=0)
    r = lax.broadcasted_iota(jnp.int32, vt.shape, 0) // N_MEM
    c = lax.broadcasted_iota(jnp.int32, vt.shape, 1) // HEAD_DIM
    blk = r == c
    vobd_ref[0, 0, :, :XW] = jnp.where(blk, vt, 0.0).astype(_BF16)
    vobd_ref[0, 0, :, XW:] = jnp.where(blk, 1.0, 0.0).astype(_BF16)


def _kv_call(mem, mem_norm, w_kv):
    depth, batch = mem_norm.shape[0], mem.shape[0]
    return pl.pallas_call(
        _kv_kernel,
        grid=(depth, batch),
        in_specs=[
            pl.BlockSpec((1, N_MEM, D_MODEL), lambda l, b: (b, 0, 0)),
            pl.BlockSpec((1, 1, D_MODEL), lambda l, b: (l, 0, 0)),
            pl.BlockSpec((1, D_MODEL, 2 * XW), lambda l, b: (l, 0, 0)),
        ],
        out_specs=[
            pl.BlockSpec((1, 1, XW, HEADS * N_MEM), lambda l, b: (l, b, 0, 0)),
            pl.BlockSpec((1, 1, HEADS * N_MEM, 2 * XW), lambda l, b: (l, b, 0, 0)),
        ],
        out_shape=[
            jax.ShapeDtypeStruct((depth, batch, XW, HEADS * N_MEM), _BF16),
            jax.ShapeDtypeStruct((depth, batch, HEADS * N_MEM, 2 * XW), _BF16),
        ],
        name="kv_proj",
    )(mem, mem_norm.reshape(depth, 1, D_MODEL), w_kv)


def _ffn_kernel(x_ref, g_ref, wgu_ref, wd_ref, fin_ref, o_ref, act_ref, *, final_norm):
    x = x_ref[...]
    h = _rms_norm(x, g_ref[...]).astype(_BF16)
    for c0, cw in FF_CHUNKS:
        gate = _dot(h, wgu_ref[:, c0:c0 + cw])
        up = _dot(h, wgu_ref[:, D_FF + c0:D_FF + c0 + cw])
        act_ref[:, c0:c0 + cw] = (gate * jax.nn.sigmoid(gate) * up).astype(_BF16)
    y = x + 0.5 * _dot(act_ref[...], wd_ref[...])
    if final_norm:
        y = _rms_norm(y, fin_ref[...])
    o_ref[...] = y


def _ffn_call(x, g, wgu, wd, fin, *, final_norm):
    t = x.shape[0]
    row = pl.BlockSpec((ROW_TILE, D_MODEL), lambda i: (i, 0))
    return pl.pallas_call(
        functools.partial(_ffn_kernel, final_norm=final_norm),
        grid=(t // ROW_TILE,),
        in_specs=[
            row,
            _resident((1, D_MODEL)),
            _resident((D_MODEL, 2 * D_FF)),
            _resident((D_FF, D_MODEL)),
            _resident((1, D_MODEL)),
        ],
        out_specs=row,
        out_shape=jax.ShapeDtypeStruct((t, D_MODEL), _F32),
        scratch_shapes=[pltpu.VMEM((ROW_TILE, D_FF), _BF16)],
        compiler_params=pltpu.CompilerParams(
            dimension_semantics=("arbitrary",), vmem_limit_bytes=VMEM_LIMIT_BYTES),
        name="ffn",
    )(x, g, wgu, wd, fin)


def _cross_attention(q, kbd_ref, vobd_ref):
    s = _dot(q.astype(_BF16), kbd_ref[0])
    es = []
    for hd in range(HEADS):
        sh = s[:, hd * N_MEM:(hd + 1) * N_MEM]
        es.append(jnp.exp(sh - jnp.max(sh, axis=-1, keepdims=True)).astype(_BF16))
    pv = _dot(jnp.concatenate(es, axis=1), vobd_ref[0])
    return pv[:, :XW] / pv[:, XW:]


def _carry_in(carry_ref, tiles_per_batch):
    @pl.when(pl.program_id(0) % tiles_per_batch == 0)
    def _():
        carry_ref[...] = jnp.zeros_like(carry_ref)


def _conv_mixer_kernel(x_ref, g_ref, win_ref, cw_ref, kbd_ref, vobd_ref, wout_ref, o_ref,
                       carry_ref, *, tiles_per_batch):
    _carry_in(carry_ref, tiles_per_batch)
    x = x_ref[...]
    h = _rms_norm(x, g_ref[...]).astype(_BF16)
    z = _dot(h, win_ref[...])
    gate_b = z[:, :MW]
    cv = z[:, MW:2 * MW] * z[:, 2 * MW:3 * MW]
    q = z[:, 3 * MW:]
    row = lax.broadcasted_iota(jnp.int32, (ROW_TILE, 1), 0)
    prev1 = carry_ref[CARRY_ROWS - 1:CARRY_ROWS, :]
    prev2 = carry_ref[CARRY_ROWS - 2:CARRY_ROWS - 1, :]
    cv1 = jnp.where(row == 0, prev1, pltpu.roll(cv, 1, axis=0))
    cv2 = jnp.where(row == 0, prev2, jnp.where(row == 1, prev1, pltpu.roll(cv, 2, axis=0)))
    carry_ref[...] = cv[ROW_TILE - CARRY_ROWS:, :]
    mix = gate_b * (cw_ref[0:1, :] * cv2 + cw_ref[1:2, :] * cv1 + cw_ref[2:3, :] * cv)
    att = _cross_attention(q, kbd_ref.at[0], vobd_ref.at[0])
    cat = jnp.concatenate([mix.astype(_BF16), att.astype(_BF16)], axis=1)
    o_ref[...] = x + _dot(cat, wout_ref[...])


def _pool_mixer_kernel(x_ref, g_ref, win_ref, wgrp_ref, sc_ref, kbd_ref, vobd_ref, wout_ref,
                       o_ref, carry_ref, *, tiles_per_batch):
    _carry_in(carry_ref, tiles_per_batch)
    x = x_ref[...]
    h = _rms_norm(x, g_ref[...]).astype(_BF16)
    z = _dot(h, win_ref[...])
    p = z[:, :MW]
    q = z[:, MW:]
    ext = jnp.concatenate([carry_ref[...], p], axis=0)
    carry_ref[...] = p[ROW_TILE - CARRY_ROWS:, :]
    lane_group = lax.broadcasted_iota(jnp.int32, (1, MW), 1) // POOL_GROUP_DIM
    win_lane = jnp.zeros((1, MW), _F32)
    acc = ext
    wsum = jnp.zeros_like(p)
    span = 1
    for gi, win in enumerate(POOL_WINDOWS):
        while span < win:
            acc = acc + pltpu.roll(acc, span, axis=0)
            span *= 2
        sel = lane_group == gi
        wsum = jnp.where(sel, acc[CARRY_ROWS:, :], wsum)
        win_lane = jnp.where(sel, float(win), win_lane)
    pos = (pl.program_id(0) % tiles_per_batch) * ROW_TILE + lax.broadcasted_iota(
        jnp.int32, (ROW_TILE, 1), 0)
    count = jnp.minimum(pos.astype(_F32) + 1.0, win_lane)
    d = (wsum / count - p).astype(_BF16)
    mix = _dot(d, wgrp_ref[...]) * sc_ref[...]
    att = _cross_attention(q, kbd_ref.at[0], vobd_ref.at[0])
    cat = jnp.concatenate([mix.astype(_BF16), att.astype(_BF16)], axis=1)
    o_ref[...] = x + _dot(cat, wout_ref[...])


def _mixer_call(kernel_fn, name, x, layer, kbd, vobd, weights_before, w_out, seq):
    t = x.shape[0]
    tiles_per_batch = seq // ROW_TILE
    row = pl.BlockSpec((ROW_TILE, D_MODEL), lambda i: (i, 0))
    kbd_spec = pl.BlockSpec((1, 1, XW, HEADS * N_MEM),
                            lambda i: (layer, i // tiles_per_batch, 0, 0))
    vobd_spec = pl.BlockSpec((1, 1, HEADS * N_MEM, 2 * XW),
                             lambda i: (layer, i // tiles_per_batch, 0, 0))
    return pl.pallas_call(
        functools.partial(kernel_fn, tiles_per_batch=tiles_per_batch),
        grid=(t // ROW_TILE,),
        in_specs=[row] + [_resident(w.shape) for w in weights_before]
        + [kbd_spec, vobd_spec, _resident(w_out.shape)],
        out_specs=row,
        out_shape=jax.ShapeDtypeStruct((t, D_MODEL), _F32),
        scratch_shapes=[pltpu.VMEM((CARRY_ROWS, MW), _F32)],
        compiler_params=pltpu.CompilerParams(
            dimension_semantics=("arbitrary",), vmem_limit_bytes=VMEM_LIMIT_BYTES),
        name=name,
    )(x, *weights_before, kbd, vobd, w_out)


def _block_diag(w_group):
    g, d, _ = w_group.shape
    out = jnp.zeros((g * d, g * d), w_group.dtype)
    for gi in range(g):
        out = lax.dynamic_update_slice(out, w_group[gi], (gi * d, gi * d))
    return out


def kernel(x, mem, ffn1_norm, ffn1_w_gu, ffn1_w_down, mix_norm, mem_norm, w_kv, w_out,
           conv_w_in, conv_w, pool_w_in, pool_w_group, pool_scale,
           ffn2_norm, ffn2_w_gu, ffn2_w_down, final_norm):
    batch, seq, _ = x.shape
    depth = ffn1_norm.shape[0]
    assert seq % ROW_TILE == 0 and ROW_TILE >= CARRY_ROWS >= max(POOL_WINDOWS) - 1

    bf = lambda w: w.astype(_BF16)
    vec = lambda v: v.reshape(1, -1)
    kbd, vobd = _kv_call(mem, mem_norm, bf(w_kv))
    fin = vec(final_norm)

    xt = x.reshape(batch * seq, D_MODEL)
    ia = ib = 0
    for i in range(depth):
        xt = _ffn_call(xt, vec(ffn1_norm[i]), bf(ffn1_w_gu[i]), bf(ffn1_w_down[i]), fin,
                       final_norm=False)
        if i % 2 == 0:
            xt = _mixer_call(_conv_mixer_kernel, "conv_mixer", xt, i, kbd, vobd,
                             [vec(mix_norm[i]), bf(conv_w_in[ia]), conv_w[ia]],
                             bf(w_out[i]), seq)
            ia += 1
        else:
            xt = _mixer_call(_pool_mixer_kernel, "pool_mixer", xt, i, kbd, vobd,
                             [vec(mix_norm[i]), bf(pool_w_in[ib]),
                              bf(_block_diag(pool_w_group[ib])), vec(pool_scale[ib])],
                             bf(w_out[i]), seq)
            ib += 1
        xt = _ffn_call(xt, vec(ffn2_norm[i]), bf(ffn2_w_gu[i]), bf(ffn2_w_down[i]), fin,
                       final_norm=(i == depth - 1))
    return xt.reshape(batch, seq, D_MODEL)
```

```python
import functools

import jax
import jax.numpy as jnp
from jax import lax
from jax.experimental import pallas as pl
from jax.experimental.pallas import tpu as pltpu

D_MODEL = 1024
N_MEM = 256
HEADS = 4
HEAD_DIM = 64
XW = HEADS * HEAD_DIM
MW = D_MODEL - XW
CONV_WIDTH = 3
POOL_WINDOWS = (2, 4, 8, 16)
POOL_GROUP_DIM = MW // len(POOL_WINDOWS)
D_FF = 2816
RMS_EPS = 1e-6

FFN_ROW_TILE = 1024
MIX_ROW_TILE = 512
CARRY_ROWS = 16
FF_CHUNKS = ((0, 512), (512, 512), (1024, 512), (1536, 512), (2048, 512), (2560, 256))
VMEM_LIMIT_BYTES = 56 * 1024 * 1024

_BF16 = jnp.bfloat16
_F32 = jnp.float32


def _rms_norm(x, g):
    y = x * lax.rsqrt(jnp.mean(x * x, axis=-1, keepdims=True) + RMS_EPS)
    return y * g


def _dot(a, b):
    return jnp.dot(a, b, preferred_element_type=_F32)


def _layer_resident(shape, layer):
    nd = len(shape) - 1
    return pl.BlockSpec((1,) + tuple(shape[1:]), lambda *_: (layer,) + (0,) * nd,
                        pipeline_mode=pl.Buffered(1))


def _kv_kernel(mem_ref, g_ref, wkv_ref, kbd_ref, vobd_ref):
    mem_h = _rms_norm(mem_ref[0], g_ref[0]).astype(_BF16)
    kv = _dot(mem_h, wkv_ref[0])
    k = kv[:, :XW] * (HEAD_DIM ** -0.5)
    v = kv[:, XW:]
    kt = jnp.concatenate([k.T] * HEADS, axis=1)
    r = lax.broadcasted_iota(jnp.int32, kt.shape, 0) // HEAD_DIM
    c = lax.broadcasted_iota(jnp.int32, kt.shape, 1) // N_MEM
    kbd_ref[0, 0] = jnp.where(r == c, kt, 0.0).astype(_BF16)
    vt = jnp.concatenate([v] * HEADS, axis=0)
    r = lax.broadcasted_iota(jnp.int32, vt.shape, 0) // N_MEM
    c = lax.broadcasted_iota(jnp.int32, vt.shape, 1) // HEAD_DIM
    blk = r == c
    vobd_ref[0, 0, :, :XW] = jnp.where(blk, vt, 0.0).astype(_BF16)
    vobd_ref[0, 0, :, XW:] = jnp.where(blk, 1.0, 0.0).astype(_BF16)


def _kv_call(mem, mem_norm, w_kv):
    depth, batch = mem_norm.shape[0], mem.shape[0]
    return pl.pallas_call(
        _kv_kernel,
        grid=(depth, batch),
        in_specs=[
            pl.BlockSpec((1, N_MEM, D_MODEL), lambda l, b: (b, 0, 0)),
            pl.BlockSpec((1, 1, D_MODEL), lambda l, b: (l, 0, 0)),
            pl.BlockSpec((1, D_MODEL, 2 * XW), lambda l, b: (l, 0, 0)),
        ],
        out_specs=[
            pl.BlockSpec((1, 1, XW, HEADS * N_MEM), lambda l, b: (l, b, 0, 0)),
            pl.BlockSpec((1, 1, HEADS * N_MEM, 2 * XW), lambda l, b: (l, b, 0, 0)),
        ],
        out_shape=[
            jax.ShapeDtypeStruct((depth, batch, XW, HEADS * N_MEM), _BF16),
            jax.ShapeDtypeStruct((depth, batch, HEADS * N_MEM, 2 * XW), _BF16),
        ],
        name="kv_proj",
    )(mem, mem_norm.reshape(depth, 1, D_MODEL), w_kv)


def _ffn_kernel(x_ref, g_ref, wgu_ref, wd_ref, fin_ref, o_ref, act_ref, *, final_norm):
    x = x_ref[...]
    h = _rms_norm(x, g_ref[0]).astype(_BF16)
    for c0, cw in FF_CHUNKS:
        gate = _dot(h, wgu_ref[0, :, c0:c0 + cw])
        up = _dot(h, wgu_ref[0, :, D_FF + c0:D_FF + c0 + cw])
        act_ref[:, c0:c0 + cw] = (gate * jax.nn.sigmoid(gate) * up).astype(_BF16)
    y = x + 0.5 * _dot(act_ref[...], wd_ref[0])
    if final_norm:
        y = _rms_norm(y, fin_ref[...])
    o_ref[...] = y


def _ffn_call(x, layer, g, wgu, wd, fin, *, final_norm):
    t = x.shape[0]
    row = pl.BlockSpec((FFN_ROW_TILE, D_MODEL), lambda i: (i, 0))
    return pl.pallas_call(
        functools.partial(_ffn_kernel, final_norm=final_norm),
        grid=(t // FFN_ROW_TILE,),
        in_specs=[
            row,
            _layer_resident(g.shape, layer),
            _layer_resident(wgu.shape, layer),
            _layer_resident(wd.shape, layer),
            pl.BlockSpec(fin.shape, lambda i: (0, 0), pipeline_mode=pl.Buffered(1)),
        ],
        out_specs=row,
        out_shape=jax.ShapeDtypeStruct((t, D_MODEL), _F32),
        scratch_shapes=[pltpu.VMEM((FFN_ROW_TILE, D_FF), _BF16)],
        compiler_params=pltpu.CompilerParams(
            dimension_semantics=("arbitrary",), vmem_limit_bytes=VMEM_LIMIT_BYTES),
        name="ffn",
    )(x, g, wgu, wd, fin)


def _cross_attention(q, kbd_ref, vobd_ref):
    s = _dot(q.astype(_BF16), kbd_ref[0, 0])
    es = []
    for hd in range(HEADS):
        sh = s[:, hd * N_MEM:(hd + 1) * N_MEM]
        es.append(jnp.exp(sh - jnp.max(sh, axis=-1, keepdims=True)).astype(_BF16))
    pv = _dot(jnp.concatenate(es, axis=1), vobd_ref[0, 0])
    return pv[:, :XW] / pv[:, XW:]


def _carry_in(carry_ref, tiles_per_batch):
    @pl.when(pl.program_id(0) % tiles_per_batch == 0)
    def _():
        carry_ref[...] = jnp.zeros_like(carry_ref)


def _conv_mixer_kernel(x_ref, g_ref, win_ref, cw_ref, kbd_ref, vobd_ref, wout_ref, o_ref,
                       carry_ref, *, tiles_per_batch):
    _carry_in(carry_ref, tiles_per_batch)
    x = x_ref[...]
    h = _rms_norm(x, g_ref[0]).astype(_BF16)
    z = _dot(h, win_ref[0])
    gate_b = z[:, :MW]
    cv = z[:, MW:2 * MW] * z[:, 2 * MW:3 * MW]
    q = z[:, 3 * MW:]
    row = lax.broadcasted_iota(jnp.int32, (MIX_ROW_TILE, 1), 0)
    prev1 = carry_ref[CARRY_ROWS - 1:CARRY_ROWS, :]
    prev2 = carry_ref[CARRY_ROWS - 2:CARRY_ROWS - 1, :]
    cv1 = jnp.where(row == 0, prev1, pltpu.roll(cv, 1, axis=0))
    cv2 = jnp.where(row == 0, prev2, jnp.where(row == 1, prev1, pltpu.roll(cv, 2, axis=0)))
    carry_ref[...] = cv[MIX_ROW_TILE - CARRY_ROWS:, :]
    mix = gate_b * (cw_ref[0, 0:1, :] * cv2 + cw_ref[0, 1:2, :] * cv1 + cw_ref[0, 2:3, :] * cv)
    att = _cross_attention(q, kbd_ref, vobd_ref)
    cat = jnp.concatenate([mix.astype(_BF16), att.astype(_BF16)], axis=1)
    o_ref[...] = x + _dot(cat, wout_ref[0])


def _pool_mixer_kernel(x_ref, g_ref, win_ref, wgrp_ref, sc_ref, kbd_ref, vobd_ref, wout_ref,
                       o_ref, carry_ref, *, tiles_per_batch):
    _carry_in(carry_ref, tiles_per_batch)
    x = x_ref[...]
    h = _rms_norm(x, g_ref[0]).astype(_BF16)
    z = _dot(h, win_ref[0])
    p = z[:, :MW]
    q = z[:, MW:]
    ext = jnp.concatenate([carry_ref[...], p], axis=0)
    carry_ref[...] = p[MIX_ROW_TILE - CARRY_ROWS:, :]
    lane_group = lax.broadcasted_iota(jnp.int32, (1, MW), 1) // POOL_GROUP_DIM
    win_lane = jnp.zeros((1, MW), _F32)
    acc = ext
    wsum = jnp.zeros_like(p)
    span = 1
    for gi, win in enumerate(POOL_WINDOWS):
        while span < win:
            acc = acc + pltpu.roll(acc, span, axis=0)
            span *= 2
        sel = lane_group == gi
        wsum = jnp.where(sel, acc[CARRY_ROWS:, :], wsum)
        win_lane = jnp.where(sel, float(win), win_lane)
    pos = (pl.program_id(0) % tiles_per_batch) * MIX_ROW_TILE + lax.broadcasted_iota(
        jnp.int32, (MIX_ROW_TILE, 1), 0)
    count = jnp.minimum(pos.astype(_F32) + 1.0, win_lane)
    d = (wsum / count - p).astype(_BF16)
    mix = _dot(d, wgrp_ref[0]) * sc_ref[0]
    att = _cross_attention(q, kbd_ref, vobd_ref)
    cat = jnp.concatenate([mix.astype(_BF16), att.astype(_BF16)], axis=1)
    o_ref[...] = x + _dot(cat, wout_ref[0])


def _mixer_call(kernel_fn, name, x, layer, kbd, vobd, weights_before, w_out, seq):
    t = x.shape[0]
    tiles_per_batch = seq // MIX_ROW_TILE
    row = pl.BlockSpec((MIX_ROW_TILE, D_MODEL), lambda i: (i, 0))
    kbd_spec = pl.BlockSpec((1, 1, XW, HEADS * N_MEM),
                            lambda i: (layer, i // tiles_per_batch, 0, 0))
    vobd_spec = pl.BlockSpec((1, 1, HEADS * N_MEM, 2 * XW),
                             lambda i: (layer, i // tiles_per_batch, 0, 0))
    return pl.pallas_call(
        functools.partial(kernel_fn, tiles_per_batch=tiles_per_batch),
        grid=(t // MIX_ROW_TILE,),
        in_specs=[row] + [_layer_resident(w.shape, li) for w, li in weights_before]
        + [kbd_spec, vobd_spec, _layer_resident(w_out.shape, layer)],
        out_specs=row,
        out_shape=jax.ShapeDtypeStruct((t, D_MODEL), _F32),
        scratch_shapes=[pltpu.VMEM((CARRY_ROWS, MW), _F32)],
        compiler_params=pltpu.CompilerParams(
            dimension_semantics=("arbitrary",), vmem_limit_bytes=VMEM_LIMIT_BYTES),
        name=name,
    )(x, *[w for w, _ in weights_before], kbd, vobd, w_out)


def _block_diag(w_group):
    nl, g, d, _ = w_group.shape
    out = jnp.zeros((nl, g * d, g * d), w_group.dtype)
    for gi in range(g):
        out = lax.dynamic_update_slice(out, w_group[:, gi], (0, gi * d, gi * d))
    return out


def kernel(x, mem, ffn1_norm, ffn1_w_gu, ffn1_w_down, mix_norm, mem_norm, w_kv, w_out,
           conv_w_in, conv_w, pool_w_in, pool_w_group, pool_scale,
           ffn2_norm, ffn2_w_gu, ffn2_w_down, final_norm):
    batch, seq, _ = x.shape
    depth = ffn1_norm.shape[0]
    assert seq % MIX_ROW_TILE == 0 and (batch * seq) % FFN_ROW_TILE == 0
    assert MIX_ROW_TILE >= CARRY_ROWS >= max(POOL_WINDOWS) - 1

    bf = lambda w: w.astype(_BF16)
    vec = lambda v: v.reshape(v.shape[0], 1, v.shape[-1])
    kbd, vobd = _kv_call(mem, mem_norm, bf(w_kv))
    fin = final_norm.reshape(1, D_MODEL)
    ffn1 = (vec(ffn1_norm), bf(ffn1_w_gu), bf(ffn1_w_down))
    ffn2 = (vec(ffn2_norm), bf(ffn2_w_gu), bf(ffn2_w_down))
    mix_g, w_out_b = vec(mix_norm), bf(w_out)
    conv_in, pool_in = bf(conv_w_in), bf(pool_w_in)
    pool_grp, pool_sc = bf(_block_diag(pool_w_group)), vec(pool_scale)

    xt = x.reshape(batch * seq, D_MODEL)
    ia = ib = 0
    for i in range(depth):
        xt = _ffn_call(xt, i, *ffn1, fin, final_norm=False)
        if i % 2 == 0:
            xt = _mixer_call(_conv_mixer_kernel, "conv_mixer", xt, i, kbd, vobd,
                             [(mix_g, i), (conv_in, ia), (conv_w, ia)], w_out_b, seq)
            ia += 1
        else:
            xt = _mixer_call(_pool_mixer_kernel, "pool_mixer", xt, i, kbd, vobd,
                             [(mix_g, i), (pool_in, ib), (pool_grp, ib), (pool_sc, ib)],
                             w_out_b, seq)
            ib += 1
        xt = _ffn_call(xt, i, *ffn2, fin, final_norm=(i == depth - 1))
    return xt.reshape(batch, seq, D_MODEL)
```

```python
import functools

import jax
import jax.numpy as jnp
from jax import lax
from jax.experimental import pallas as pl
from jax.experimental.pallas import tpu as pltpu

D_MODEL = 1024
N_MEM = 256
HEADS = 4
HEAD_DIM = 64
XW = HEADS * HEAD_DIM
MW = D_MODEL - XW
CONV_WIDTH = 3
POOL_WINDOWS = (2, 4, 8, 16)
POOL_GROUP_DIM = MW // len(POOL_WINDOWS)
D_FF = 2816
RMS_EPS = 1e-6

ROW_TILE = 512
SUB_TILES = 2
STEP_ROWS = ROW_TILE * SUB_TILES
CARRY_ROWS = 16
FF_CHUNKS = ((0, 1024), (1024, 1024), (2048, 768))
VMEM_LIMIT_BYTES = 56 * 1024 * 1024

_BF16 = jnp.bfloat16
_F32 = jnp.float32


def _rms_norm(x, g):
    y = x * lax.rsqrt(jnp.mean(x * x, axis=-1, keepdims=True) + RMS_EPS)
    return y * g


def _dot(a, b):
    return jnp.dot(a, b, preferred_element_type=_F32)


def _layer_resident(shape, layer):
    nd = len(shape) - 1
    return pl.BlockSpec((1,) + tuple(shape[1:]), lambda *_: (layer,) + (0,) * nd,
                        pipeline_mode=pl.Buffered(1))


def _row_specs(total_rows):
    last_tile = total_rows // ROW_TILE - 1
    step = pl.BlockSpec((STEP_ROWS, D_MODEL), lambda i: (i, 0))
    nxt = pl.BlockSpec((ROW_TILE, D_MODEL),
                       lambda i: (jnp.minimum((i + 1) * SUB_TILES, last_tile), 0))
    return step, nxt


def _zero_bits_of(h):
    u = pltpu.bitcast(h, jnp.uint32)
    parts = [u[r:r + 8, :] for r in range(0, u.shape[0], 8)]
    while len(parts) > 1:
        parts = [a | b for a, b in zip(parts[0::2], parts[1::2])]
    half = jnp.uint32(16)
    return lax.shift_right_logical(lax.shift_right_logical(parts[0], half), half)


def _after(x, zero_bits):
    if zero_bits is None:
        return x
    zero = lax.bitcast_convert_type(zero_bits[0:1, :x.shape[1]], jnp.int32).astype(_F32)
    return x + zero


def _for_each_tile(x_ref, xn_ref, g, h_ref, tile_fn, carry_ref=None):
    def norm_into(slot, x):
        h = _rms_norm(x, g).astype(_BF16)
        h_ref[slot] = h
        return h

    @pl.when(pl.program_id(0) == 0)
    def _():
        norm_into(0, x_ref[0:ROW_TILE, :])
        if carry_ref is not None:
            carry_ref[...] = jnp.zeros_like(carry_ref)

    for s in range(SUB_TILES):
        zero_bits = None
        if s + 1 < SUB_TILES:
            norm_into((s + 1) % 2, x_ref[(s + 1) * ROW_TILE:(s + 2) * ROW_TILE, :])
        else:
            zero_bits = _zero_bits_of(norm_into((s + 1) % 2, xn_ref[...]))
        tile_fn(s, pl.ds(s * ROW_TILE, ROW_TILE), s % 2, zero_bits)


def _kv_kernel(mem_ref, g_ref, wkv_ref, kbd_ref, vobd_ref):
    mem_h = _rms_norm(mem_ref[0], g_ref[0]).astype(_BF16)
    kv = _dot(mem_h, wkv_ref[0])
    k = kv[:, :XW] * (HEAD_DIM ** -0.5)
    v = kv[:, XW:]
    kt = jnp.concatenate([k.T] * HEADS, axis=1)
    r = lax.broadcasted_iota(jnp.int32, kt.shape, 0) // HEAD_DIM
    c = lax.broadcasted_iota(jnp.int32, kt.shape, 1) // N_MEM
    kbd_ref[0, 0] = jnp.where(r == c, kt, 0.0).astype(_BF16)
    vt = jnp.concatenate([v] * HEADS, axis=0)
    r = lax.broadcasted_iota(jnp.int32, vt.shape, 0) // N_MEM
    c = lax.broadcasted_iota(jnp.int32, vt.shape, 1) // HEAD_DIM
    blk = r == c
    vobd_ref[0, 0, :, :XW] = jnp.where(blk, vt, 0.0).astype(_BF16)
    vobd_ref[0, 0, :, XW:] = jnp.where(blk, 1.0, 0.0).astype(_BF16)


def _kv_call(mem, mem_norm, w_kv):
    depth, batch = mem_norm.shape[0], mem.shape[0]
    return pl.pallas_call(
        _kv_kernel,
        grid=(depth, batch),
        in_specs=[
            pl.BlockSpec((1, N_MEM, D_MODEL), lambda l, b: (b, 0, 0)),
            pl.BlockSpec((1, 1, D_MODEL), lambda l, b: (l, 0, 0)),
            pl.BlockSpec((1, D_MODEL, 2 * XW), lambda l, b: (l, 0, 0)),
        ],
        out_specs=[
            pl.BlockSpec((1, 1, XW, HEADS * N_MEM), lambda l, b: (l, b, 0, 0)),
            pl.BlockSpec((1, 1, HEADS * N_MEM, 2 * XW), lambda l, b: (l, b, 0, 0)),
        ],
        out_shape=[
            jax.ShapeDtypeStruct((depth, batch, XW, HEADS * N_MEM), _BF16),
            jax.ShapeDtypeStruct((depth, batch, HEADS * N_MEM, 2 * XW), _BF16),
        ],
        name="kv_proj",
    )(mem, mem_norm.reshape(depth, 1, D_MODEL), w_kv)


def _ffn_kernel(x_ref, xn_ref, g_ref, wgu_ref, wd_ref, fin_ref, o_ref, h_ref, act_ref, *,
                final_norm):
    def tile(s, rows, slot, zero_bits):
        for c0, cw in FF_CHUNKS:
            gate = _dot(h_ref[slot], wgu_ref[0, :, c0:c0 + cw])
            if (c0, cw) == FF_CHUNKS[-1]:
                gate = _after(gate, zero_bits)
            up = _dot(h_ref[slot], wgu_ref[0, :, D_FF + c0:D_FF + c0 + cw])
            act_ref[s, :, c0:c0 + cw] = (gate * jax.nn.sigmoid(gate) * up).astype(_BF16)
        y = x_ref[rows, :] + 0.5 * _dot(act_ref[s], wd_ref[0])
        if final_norm:
            y = _rms_norm(y, fin_ref[...])
        o_ref[rows, :] = y

    _for_each_tile(x_ref, xn_ref, g_ref[0], h_ref, tile)


def _ffn_call(x, layer, g, wgu, wd, fin, *, final_norm):
    t = x.shape[0]
    step, nxt = _row_specs(t)
    return pl.pallas_call(
        functools.partial(_ffn_kernel, final_norm=final_norm),
        grid=(t // STEP_ROWS,),
        in_specs=[
            step, nxt,
            _layer_resident(g.shape, layer),
            _layer_resident(wgu.shape, layer),
            _layer_resident(wd.shape, layer),
            pl.BlockSpec(fin.shape, lambda i: (0, 0), pipeline_mode=pl.Buffered(1)),
        ],
        out_specs=step,
        out_shape=jax.ShapeDtypeStruct((t, D_MODEL), _F32),
        scratch_shapes=[pltpu.VMEM((2, ROW_TILE, D_MODEL), _BF16),
                        pltpu.VMEM((SUB_TILES, ROW_TILE, D_FF), _BF16)],
        compiler_params=pltpu.CompilerParams(
            dimension_semantics=("arbitrary",), vmem_limit_bytes=VMEM_LIMIT_BYTES),
        name="ffn",
    )(x, x, g, wgu, wd, fin)


def _cross_attention(q, kbd_ref, vobd_ref):
    s = _dot(q.astype(_BF16), kbd_ref[0, 0])
    es = []
    for hd in range(HEADS):
        sh = s[:, hd * N_MEM:(hd + 1) * N_MEM]
        es.append(jnp.exp(sh - jnp.max(sh, axis=-1, keepdims=True)).astype(_BF16))
    pv = _dot(jnp.concatenate(es, axis=1), vobd_ref[0, 0])
    return pv[:, :XW] / pv[:, XW:]


def _tile_in_batch(s, tiles_per_batch):
    return (pl.program_id(0) * SUB_TILES + s) % tiles_per_batch


def _carry_in(carry_ref, tile_in_batch):
    return jnp.where(tile_in_batch == 0, 0.0, carry_ref[...])


def _conv_mixer_kernel(x_ref, xn_ref, g_ref, win_ref, cw_ref, kbd_ref, vobd_ref, wout_ref,
                       o_ref, h_ref, carry_ref, *, tiles_per_batch):
    def tile(s, rows, slot, zero_bits):
        carry = _carry_in(carry_ref, _tile_in_batch(s, tiles_per_batch))
        z = _dot(h_ref[slot], win_ref[0])
        gate_b = z[:, :MW]
        cv = z[:, MW:2 * MW] * z[:, 2 * MW:3 * MW]
        q = z[:, 3 * MW:]
        row = lax.broadcasted_iota(jnp.int32, (ROW_TILE, 1), 0)
        prev1 = carry[CARRY_ROWS - 1:CARRY_ROWS, :]
        prev2 = carry[CARRY_ROWS - 2:CARRY_ROWS - 1, :]
        cv1 = jnp.where(row == 0, prev1, pltpu.roll(cv, 1, axis=0))
        cv2 = jnp.where(row == 0, prev2,
                        jnp.where(row == 1, prev1, pltpu.roll(cv, 2, axis=0)))
        carry_ref[...] = cv[ROW_TILE - CARRY_ROWS:, :]
        mix = gate_b * (cw_ref[0, 0:1, :] * cv2 + cw_ref[0, 1:2, :] * cv1
                        + cw_ref[0, 2:3, :] * cv)
        att = _cross_attention(_after(q, zero_bits), kbd_ref, vobd_ref)
        cat = jnp.concatenate([mix.astype(_BF16), att.astype(_BF16)], axis=1)
        o_ref[rows, :] = x_ref[rows, :] + _dot(cat, wout_ref[0])

    _for_each_tile(x_ref, xn_ref, g_ref[0], h_ref, tile, carry_ref)


def _pool_mixer_kernel(x_ref, xn_ref, g_ref, win_ref, wgrp_ref, sc_ref, kbd_ref, vobd_ref,
                       wout_ref, o_ref, h_ref, carry_ref, *, tiles_per_batch):
    def tile(s, rows, slot, zero_bits):
        tile_in_batch = _tile_in_batch(s, tiles_per_batch)
        carry = _carry_in(carry_ref, tile_in_batch)
        z = _dot(h_ref[slot], win_ref[0])
        p = z[:, :MW]
        q = z[:, MW:]
        ext = jnp.concatenate([carry, p], axis=0)
        carry_ref[...] = p[ROW_TILE - CARRY_ROWS:, :]
        lane_group = lax.broadcasted_iota(jnp.int32, (1, MW), 1) // POOL_GROUP_DIM
        win_lane = jnp.zeros((1, MW), _F32)
        acc = ext
        wsum = jnp.zeros_like(p)
        span = 1
        for gi, win in enumerate(POOL_WINDOWS):
            while span < win:
                acc = acc + pltpu.roll(acc, span, axis=0)
                span *= 2
            sel = lane_group == gi
            wsum = jnp.where(sel, acc[CARRY_ROWS:, :], wsum)
            win_lane = jnp.where(sel, float(win), win_lane)
        pos = tile_in_batch * ROW_TILE + lax.broadcasted_iota(jnp.int32, (ROW_TILE, 1), 0)
        count = jnp.minimum(pos.astype(_F32) + 1.0, win_lane)
        d = (wsum / count - p).astype(_BF16)
        mix = _dot(d, wgrp_ref[0]) * sc_ref[0]
        att = _cross_attention(_after(q, zero_bits), kbd_ref, vobd_ref)
        cat = jnp.concatenate([mix.astype(_BF16), att.astype(_BF16)], axis=1)
        o_ref[rows, :] = x_ref[rows, :] + _dot(cat, wout_ref[0])

    _for_each_tile(x_ref, xn_ref, g_ref[0], h_ref, tile, carry_ref)


def _mixer_call(kernel_fn, name, x, layer, kbd, vobd, weights_before, w_out, seq):
    t = x.shape[0]
    tiles_per_batch = seq // ROW_TILE
    steps_per_batch = seq // STEP_ROWS
    step, nxt = _row_specs(t)
    kbd_spec = pl.BlockSpec((1, 1, XW, HEADS * N_MEM),
                            lambda i: (layer, i // steps_per_batch, 0, 0))
    vobd_spec = pl.BlockSpec((1, 1, HEADS * N_MEM, 2 * XW),
                             lambda i: (layer, i // steps_per_batch, 0, 0))
    return pl.pallas_call(
        functools.partial(kernel_fn, tiles_per_batch=tiles_per_batch),
        grid=(t // STEP_ROWS,),
        in_specs=[step, nxt] + [_layer_resident(w.shape, li) for w, li in weights_before]
        + [kbd_spec, vobd_spec, _layer_resident(w_out.shape, layer)],
        out_specs=step,
        out_shape=jax.ShapeDtypeStruct((t, D_MODEL), _F32),
        scratch_shapes=[pltpu.VMEM((2, ROW_TILE, D_MODEL), _BF16),
                        pltpu.VMEM((CARRY_ROWS, MW), _F32)],
        compiler_params=pltpu.CompilerParams(
            dimension_semantics=("arbitrary",), vmem_limit_bytes=VMEM_LIMIT_BYTES),
        name=name,
    )(x, x, *[w for w, _ in weights_before], kbd, vobd, w_out)


def _block_diag(w_group):
    nl, g, d, _ = w_group.shape
    out = jnp.zeros((nl, g * d, g * d), w_group.dtype)
    for gi in range(g):
        out = lax.dynamic_update_slice(out, w_group[:, gi], (0, gi * d, gi * d))
    return out


def kernel(x, mem, ffn1_norm, ffn1_w_gu, ffn1_w_down, mix_norm, mem_norm, w_kv, w_out,
           conv_w_in, conv_w, pool_w_in, pool_w_group, pool_scale,
           ffn2_norm, ffn2_w_gu, ffn2_w_down, final_norm):
    batch, seq, _ = x.shape
    depth = ffn1_norm.shape[0]
    assert seq % STEP_ROWS == 0 and SUB_TILES % 2 == 0
    assert ROW_TILE >= CARRY_ROWS >= max(POOL_WINDOWS) - 1

    bf = lambda w: w.astype(_BF16)
    vec = lambda v: v.reshape(v.shape[0], 1, v.shape[-1])
    kbd, vobd = _kv_call(mem, mem_norm, bf(w_kv))
    fin = final_norm.reshape(1, D_MODEL)
    ffn1 = (vec(ffn1_norm), bf(ffn1_w_gu), bf(ffn1_w_down))
    ffn2 = (vec(ffn2_norm), bf(ffn2_w_gu), bf(ffn2_w_down))
    mix_g, w_out_b = vec(mix_norm), bf(w_out)
    conv_in, pool_in = bf(conv_w_in), bf(pool_w_in)
    pool_grp, pool_sc = bf(_block_diag(pool_w_group)), vec(pool_scale)

    xt = x.reshape(batch * seq, D_MODEL)
    ia = ib = 0
    for i in range(depth):
        xt = _ffn_call(xt, i, *ffn1, fin, final_norm=False)
        if i % 2 == 0:
            xt = _mixer_call(_conv_mixer_kernel, "conv_mixer", xt, i, kbd, vobd,
                             [(mix_g, i), (conv_in, ia), (conv_w, ia)], w_out_b, seq)
            ia += 1
        else:
            xt = _mixer_call(_pool_mixer_kernel, "pool_mixer", xt, i, kbd, vobd,
                             [(mix_g, i), (pool_in, ib), (pool_grp, ib), (pool_sc, ib)],
                             w_out_b, seq)
            ib += 1
        xt = _ffn_call(xt, i, *ffn2, fin, final_norm=(i == depth - 1))
    return xt.reshape(batch, seq, D_MODEL)
```

```python
import functools

import jax
import jax.numpy as jnp
from jax import lax
from jax.experimental import pallas as pl
from jax.experimental.pallas import tpu as pltpu

D_MODEL = 1024
N_MEM = 256
HEADS = 4
HEAD_DIM = 64
XW = HEADS * HEAD_DIM
MW = D_MODEL - XW
CONV_WIDTH = 3
POOL_WINDOWS = (2, 4, 8, 16)
POOL_GROUP_DIM = MW // len(POOL_WINDOWS)
D_FF = 2816
RMS_EPS = 1e-6

FFN_STEP_ROWS = 1024
FF_CHUNKS = ((0, 512), (512, 512), (1024, 512), (1536, 512), (2048, 512), (2560, 256))
ROW_TILE = 512
SUB_TILES = 2
STEP_ROWS = ROW_TILE * SUB_TILES
CARRY_ROWS = 16
BF16_SUBLANES = 16
VMEM_LIMIT_BYTES = 56 * 1024 * 1024

_BF16 = jnp.bfloat16
_F32 = jnp.float32


def _rms_norm(x, g):
    y = x * lax.rsqrt(jnp.mean(x * x, axis=-1, keepdims=True) + RMS_EPS)
    return y * g


def _dot(a, b):
    return jnp.dot(a, b, preferred_element_type=_F32)


def _layer_resident(shape, layer):
    nd = len(shape) - 1
    return pl.BlockSpec((1,) + tuple(shape[1:]), lambda *_: (layer,) + (0,) * nd,
                        pipeline_mode=pl.Buffered(1))


def _kv_kernel(mem_ref, g_ref, wkv_ref, kbd_ref, vobd_ref):
    mem_h = _rms_norm(mem_ref[0], g_ref[0])
    kv = _dot(mem_h, wkv_ref[0])
    k = kv[:, :XW] * (HEAD_DIM ** -0.5)
    v = kv[:, XW:]
    kt = jnp.concatenate([k.T] * HEADS, axis=1)
    r = lax.broadcasted_iota(jnp.int32, kt.shape, 0) // HEAD_DIM
    c = lax.broadcasted_iota(jnp.int32, kt.shape, 1) // N_MEM
    kbd_ref[0, 0] = jnp.where(r == c, kt, 0.0).astype(_BF16)
    vt = jnp.concatenate([v] * HEADS, axis=0)
    r = lax.broadcasted_iota(jnp.int32, vt.shape, 0) // N_MEM
    c = lax.broadcasted_iota(jnp.int32, vt.shape, 1) // HEAD_DIM
    blk = r == c
    vobd_ref[0, 0, :, :XW] = jnp.where(blk, vt, 0.0).astype(_BF16)
    vobd_ref[0, 0, :, XW:] = jnp.where(blk, 1.0, 0.0).astype(_BF16)


def _kv_call(mem, mem_norm, w_kv):
    depth, batch = mem_norm.shape[0], mem.shape[0]
    return pl.pallas_call(
        _kv_kernel,
        grid=(depth, batch),
        in_specs=[
            pl.BlockSpec((1, N_MEM, D_MODEL), lambda l, b: (b, 0, 0)),
            pl.BlockSpec((1, 1, D_MODEL), lambda l, b: (l, 0, 0)),
            pl.BlockSpec((1, D_MODEL, 2 * XW), lambda l, b: (l, 0, 0)),
        ],
        out_specs=[
            pl.BlockSpec((1, 1, XW, HEADS * N_MEM), lambda l, b: (l, b, 0, 0)),
            pl.BlockSpec((1, 1, HEADS * N_MEM, 2 * XW), lambda l, b: (l, b, 0, 0)),
        ],
        out_shape=[
            jax.ShapeDtypeStruct((depth, batch, XW, HEADS * N_MEM), _BF16),
            jax.ShapeDtypeStruct((depth, batch, HEADS * N_MEM, 2 * XW), _BF16),
        ],
        name="kv_proj",
    )(mem, mem_norm.reshape(depth, 1, D_MODEL), w_kv)


def _ffn_kernel(x_ref, g_ref, wgu_ref, wd_ref, fin_ref, o_ref, act_ref, *, final_norm):
    x = x_ref[...]
    h = _rms_norm(x, g_ref[0]).astype(_BF16)
    for c0, cw in FF_CHUNKS:
        gate = _dot(h, wgu_ref[0, :, c0:c0 + cw])
        up = _dot(h, wgu_ref[0, :, D_FF + c0:D_FF + c0 + cw])
        act_ref[:, c0:c0 + cw] = (gate * jax.nn.sigmoid(gate) * up).astype(_BF16)
    y = x + 0.5 * _dot(act_ref[...], wd_ref[0])
    if final_norm:
        y = _rms_norm(y, fin_ref[...])
    o_ref[...] = y


def _ffn_call(x, g, g_layer, wgu, wd, w_layer, fin, *, final_norm):
    t = x.shape[0]
    row = pl.BlockSpec((FFN_STEP_ROWS, D_MODEL), lambda i: (i, 0))
    return pl.pallas_call(
        functools.partial(_ffn_kernel, final_norm=final_norm),
        grid=(t // FFN_STEP_ROWS,),
        in_specs=[
            row,
            _layer_resident(g.shape, g_layer),
            _layer_resident(wgu.shape, w_layer),
            _layer_resident(wd.shape, w_layer),
            pl.BlockSpec(fin.shape, lambda i: (0, 0), pipeline_mode=pl.Buffered(1)),
        ],
        out_specs=row,
        out_shape=jax.ShapeDtypeStruct((t, D_MODEL), _F32),
        scratch_shapes=[pltpu.VMEM((FFN_STEP_ROWS, D_FF), _BF16)],
        compiler_params=pltpu.CompilerParams(
            dimension_semantics=("arbitrary",), vmem_limit_bytes=VMEM_LIMIT_BYTES),
        name="ffn",
    )(x, g, wgu, wd, fin)


def _row_specs(total_rows):
    last_tile = total_rows // ROW_TILE - 1
    step = pl.BlockSpec((STEP_ROWS, D_MODEL), lambda i: (i, 0))
    nxt = pl.BlockSpec((ROW_TILE, D_MODEL),
                       lambda i: (jnp.minimum((i + 1) * SUB_TILES, last_tile), 0))
    return step, nxt


def _zero_bits_of(h):
    u = pltpu.bitcast(h, jnp.uint32)
    parts = [u[r:r + 8, :] for r in range(0, u.shape[0], 8)]
    while len(parts) > 1:
        parts = [a | b for a, b in zip(parts[0::2], parts[1::2])]
    half = jnp.uint32(16)
    return lax.shift_right_logical(lax.shift_right_logical(parts[0], half), half)


def _after(x, zero_bits):
    if zero_bits is None:
        return x
    zero = lax.bitcast_convert_type(zero_bits[0:1, :x.shape[1]], jnp.int32).astype(_F32)
    return x + zero


def _for_each_tile(x_ref, xn_ref, g, h_ref, carry_ref, tile_fn):
    def norm_into(slot, x):
        h = _rms_norm(x, g).astype(_BF16)
        h_ref[slot] = h
        return h

    @pl.when(pl.program_id(0) == 0)
    def _():
        norm_into(0, x_ref[0:ROW_TILE, :])
        carry_ref[...] = jnp.zeros_like(carry_ref)

    for s in range(SUB_TILES):
        zero_bits = None
        if s + 1 < SUB_TILES:
            norm_into((s + 1) % 2, x_ref[(s + 1) * ROW_TILE:(s + 2) * ROW_TILE, :])
        else:
            zero_bits = _zero_bits_of(norm_into((s + 1) % 2, xn_ref[...]))
        tile_fn(s, pl.ds(s * ROW_TILE, ROW_TILE), s % 2, zero_bits)


def _cast_slices(refs, n_cast):
    for src, dst in zip(refs[:n_cast], refs[n_cast + 1:2 * n_cast + 1]):
        dst[...] = src[...].astype(_BF16)


def _cross_attention(q, kbd_ref, vobd_ref):
    s = _dot(q.astype(_BF16), kbd_ref[0, 0])
    es = []
    for hd in range(HEADS):
        sh = s[:, hd * N_MEM:(hd + 1) * N_MEM]
        es.append(jnp.exp(sh - jnp.max(sh, axis=-1, keepdims=True)).astype(_BF16))
    pv = _dot(jnp.concatenate(es, axis=1), vobd_ref[0, 0])
    return pv[:, :XW] / pv[:, XW:]


def _tile_in_batch(s, tiles_per_batch):
    return (pl.program_id(0) * SUB_TILES + s) % tiles_per_batch


def _carry_in(carry_ref, tile_in_batch):
    return jnp.where(tile_in_batch == 0, 0.0, carry_ref[...])


def _conv_mixer_kernel(x_ref, xn_ref, g_ref, win_ref, cw_ref, kbd_ref, vobd_ref, wout_ref,
                       *refs, tiles_per_batch, n_cast):
    o_ref, (h_ref, carry_ref) = refs[n_cast], refs[2 * n_cast + 1:]
    _cast_slices(refs, n_cast)

    def tile(s, rows, slot, zero_bits):
        carry = _carry_in(carry_ref, _tile_in_batch(s, tiles_per_batch))
        z = _dot(h_ref[slot], win_ref[0])
        gate_b = z[:, :MW]
        cv = z[:, MW:2 * MW] * z[:, 2 * MW:3 * MW]
        q = z[:, 3 * MW:]
        row = lax.broadcasted_iota(jnp.int32, (ROW_TILE, 1), 0)
        prev1 = carry[CARRY_ROWS - 1:CARRY_ROWS, :]
        prev2 = carry[CARRY_ROWS - 2:CARRY_ROWS - 1, :]
        cv1 = jnp.where(row == 0, prev1, pltpu.roll(cv, 1, axis=0))
        cv2 = jnp.where(row == 0, prev2,
                        jnp.where(row == 1, prev1, pltpu.roll(cv, 2, axis=0)))
        carry_ref[...] = cv[ROW_TILE - CARRY_ROWS:, :]
        mix = gate_b * (cw_ref[0, 0:1, :] * cv2 + cw_ref[0, 1:2, :] * cv1
                        + cw_ref[0, 2:3, :] * cv)
        att = _cross_attention(_after(q, zero_bits), kbd_ref, vobd_ref)
        cat = jnp.concatenate([mix.astype(_BF16), att.astype(_BF16)], axis=1)
        o_ref[rows, :] = x_ref[rows, :] + _dot(cat, wout_ref[0])

    _for_each_tile(x_ref, xn_ref, g_ref[0], h_ref, carry_ref, tile)


def _pool_mixer_kernel(x_ref, xn_ref, g_ref, win_ref, wgrp_ref, sc_ref, kbd_ref, vobd_ref,
                       wout_ref, *refs, tiles_per_batch, n_cast):
    o_ref, (h_ref, carry_ref) = refs[n_cast], refs[2 * n_cast + 1:]
    _cast_slices(refs, n_cast)

    def tile(s, rows, slot, zero_bits):
        tile_in_batch = _tile_in_batch(s, tiles_per_batch)
        carry = _carry_in(carry_ref, tile_in_batch)
        z = _dot(h_ref[slot], win_ref[0])
        p = z[:, :MW]
        q = z[:, MW:]
        ext = jnp.concatenate([carry, p], axis=0)
        carry_ref[...] = p[ROW_TILE - CARRY_ROWS:, :]
        lane_group = lax.broadcasted_iota(jnp.int32, (1, MW), 1) // POOL_GROUP_DIM
        inv_win = jnp.zeros((1, MW), _F32)
        acc = ext
        wsum = jnp.zeros_like(p)
        span = 1
        for gi, win in enumerate(POOL_WINDOWS):
            while span < win:
                acc = acc + pltpu.roll(acc, span, axis=0)
                span *= 2
            sel = lane_group == gi
            wsum = jnp.where(sel, acc[CARRY_ROWS:, :], wsum)
            inv_win = jnp.where(sel, 1.0 / win, inv_win)
        pos = tile_in_batch * ROW_TILE + lax.broadcasted_iota(jnp.int32, (ROW_TILE, 1), 0)
        inv_count = jnp.maximum(1.0 / (pos.astype(_F32) + 1.0), inv_win)
        d = (wsum * inv_count - p).astype(_BF16)
        mix = _dot(d, wgrp_ref[0]) * sc_ref[0]
        att = _cross_attention(_after(q, zero_bits), kbd_ref, vobd_ref)
        cat = jnp.concatenate([mix.astype(_BF16), att.astype(_BF16)], axis=1)
        o_ref[rows, :] = x_ref[rows, :] + _dot(cat, wout_ref[0])

    _for_each_tile(x_ref, xn_ref, g_ref[0], h_ref, carry_ref, tile)


def _cast_job(w, layer, n_steps):
    _, r, c = w.shape
    n_blocks = n_steps
    while r % n_blocks or (r // n_blocks) % BF16_SUBLANES:
        n_blocks //= 2
    block = (1, r // n_blocks, c)
    src = pl.BlockSpec(block, lambda i: (layer, i * n_blocks // n_steps, 0))
    dst = pl.BlockSpec(block, lambda i: (0, i * n_blocks // n_steps, 0))
    return src, dst, jax.ShapeDtypeStruct((1, r, c), _BF16)


def _mixer_call(kernel_fn, name, x, layer, kbd, vobd, weights_before, w_out, seq, casts):
    t = x.shape[0]
    n_steps = t // STEP_ROWS
    tiles_per_batch = seq // ROW_TILE
    steps_per_batch = seq // STEP_ROWS
    step, nxt = _row_specs(t)
    kbd_spec = pl.BlockSpec((1, 1, XW, HEADS * N_MEM),
                            lambda i: (layer, i // steps_per_batch, 0, 0))
    vobd_spec = pl.BlockSpec((1, 1, HEADS * N_MEM, 2 * XW),
                             lambda i: (layer, i // steps_per_batch, 0, 0))
    jobs = [_cast_job(w, li, n_steps) for w, li in casts]
    outs = pl.pallas_call(
        functools.partial(kernel_fn, tiles_per_batch=tiles_per_batch, n_cast=len(casts)),
        grid=(n_steps,),
        in_specs=[step, nxt] + [_layer_resident(w.shape, li) for w, li in weights_before]
        + [kbd_spec, vobd_spec, _layer_resident(w_out.shape, layer)] + [j[0] for j in jobs],
        out_specs=[step] + [j[1] for j in jobs],
        out_shape=[jax.ShapeDtypeStruct((t, D_MODEL), _F32)] + [j[2] for j in jobs],
        scratch_shapes=[pltpu.VMEM((2, ROW_TILE, D_MODEL), _BF16),
                        pltpu.VMEM((CARRY_ROWS, MW), _F32)],
        compiler_params=pltpu.CompilerParams(
            dimension_semantics=("arbitrary",), vmem_limit_bytes=VMEM_LIMIT_BYTES),
        name=name,
    )(x, x, *[w for w, _ in weights_before], kbd, vobd, w_out, *[w for w, _ in casts])
    return outs[0], outs[1:]


def _block_diag(w_group):
    nl, g, d, _ = w_group.shape
    out = jnp.zeros((nl, g * d, g * d), w_group.dtype)
    for gi in range(g):
        out = lax.dynamic_update_slice(out, w_group[:, gi], (0, gi * d, gi * d))
    return out


def kernel(x, mem, ffn1_norm, ffn1_w_gu, ffn1_w_down, mix_norm, mem_norm, w_kv, w_out,
           conv_w_in, conv_w, pool_w_in, pool_w_group, pool_scale,
           ffn2_norm, ffn2_w_gu, ffn2_w_down, final_norm):
    batch, seq, _ = x.shape
    depth = ffn1_norm.shape[0]
    assert seq % STEP_ROWS == 0 and SUB_TILES % 2 == 0 and (batch * seq) % FFN_STEP_ROWS == 0
    assert ROW_TILE >= CARRY_ROWS >= max(POOL_WINDOWS) - 1

    bf = lambda w: w.astype(_BF16)
    vec = lambda v: v.reshape(v.shape[0], 1, v.shape[-1])
    kbd, vobd = _kv_call(mem, mem_norm, w_kv)
    fin = final_norm.reshape(1, D_MODEL)
    g1, g2, mix_g = vec(ffn1_norm), vec(ffn2_norm), vec(mix_norm)
    w_out_b, conv_in, pool_in = bf(w_out), bf(conv_w_in), bf(pool_w_in)
    pool_grp, pool_sc = bf(_block_diag(pool_w_group)), vec(pool_scale)

    xt = x.reshape(batch * seq, D_MODEL)
    wgu, wd, w_layer = bf(ffn1_w_gu[0:1]), bf(ffn1_w_down[0:1]), 0
    ia = ib = 0
    for i in range(depth):
        xt = _ffn_call(xt, g1, i, wgu, wd, w_layer, fin, final_norm=False)
        casts = [(ffn2_w_gu, i), (ffn2_w_down, i)]
        if i + 1 < depth:
            casts += [(ffn1_w_gu, i + 1), (ffn1_w_down, i + 1)]
        if i % 2 == 0:
            xt, cast = _mixer_call(_conv_mixer_kernel, "conv_mixer", xt, i, kbd, vobd,
                                   [(mix_g, i), (conv_in, ia), (conv_w, ia)], w_out_b, seq,
                                   casts)
            ia += 1
        else:
            xt, cast = _mixer_call(_pool_mixer_kernel, "pool_mixer", xt, i, kbd, vobd,
                                   [(mix_g, i), (pool_in, ib), (pool_grp, ib), (pool_sc, ib)],
                                   w_out_b, seq, casts)
            ib += 1
        xt = _ffn_call(xt, g2, i, cast[0], cast[1], 0, fin, final_norm=(i == depth - 1))
        if i + 1 < depth:
            wgu, wd, w_layer = cast[2], cast[3], 0
    return xt.reshape(batch, seq, D_MODEL)
```

```python
import functools

import jax
import jax.numpy as jnp
from jax import lax
from jax.experimental import pallas as pl
from jax.experimental.pallas import tpu as pltpu

D_MODEL = 1024
N_MEM = 256
HEADS = 4
HEAD_DIM = 64
XW = HEADS * HEAD_DIM
MW = D_MODEL - XW
CONV_WIDTH = 3
POOL_WINDOWS = (2, 4, 8, 16)
POOL_GROUP_DIM = MW // len(POOL_WINDOWS)
D_FF = 2816
RMS_EPS = 1e-6

FFN_STEP_ROWS = 1024
FF_CHUNKS = ((0, 512), (512, 512), (1024, 512), (1536, 512), (2048, 512), (2560, 256))
ROW_TILE = 512
SUB_TILES = 2
STEP_ROWS = ROW_TILE * SUB_TILES
CARRY_ROWS = 16
BF16_SUBLANES = 16
LANES = 128
VMEM_LIMIT_BYTES = 56 * 1024 * 1024

_BF16 = jnp.bfloat16
_F32 = jnp.float32


def _rms_norm(x, g):
    y = x * lax.rsqrt(jnp.mean(x * x, axis=-1, keepdims=True) + RMS_EPS)
    return y * g


def _dot(a, b):
    return jnp.dot(a, b, preferred_element_type=_F32)


def _layer_resident(shape, layer):
    nd = len(shape) - 1
    return pl.BlockSpec((1,) + tuple(shape[1:]), lambda *_: (layer,) + (0,) * nd,
                        pipeline_mode=pl.Buffered(1))


def _kv_kernel(mem_ref, g_ref, wkv_ref, kbd_ref, vbd_ref):
    mem_h = _rms_norm(mem_ref[0], g_ref[0])
    kv = _dot(mem_h, wkv_ref[0])
    k = kv[:, :XW] * (HEAD_DIM ** -0.5)
    v = kv[:, XW:]
    kt = jnp.concatenate([k.T] * HEADS, axis=1)
    r = lax.broadcasted_iota(jnp.int32, kt.shape, 0) // HEAD_DIM
    c = lax.broadcasted_iota(jnp.int32, kt.shape, 1) // N_MEM
    kbd_ref[0, 0] = jnp.where(r == c, kt, 0.0).astype(_BF16)
    vt = jnp.concatenate([v] * HEADS, axis=0)
    r = lax.broadcasted_iota(jnp.int32, vt.shape, 0) // N_MEM
    c = lax.broadcasted_iota(jnp.int32, vt.shape, 1) // HEAD_DIM
    blk = r == c
    vbd_ref[0, 0, :, :XW] = jnp.where(blk, vt, 0.0).astype(_BF16)
    vbd_ref[0, 0, :, XW:] = jnp.where(blk, 1.0, 0.0).astype(_BF16)


def _kv_call(mem, mem_norm, w_kv):
    depth, batch = mem_norm.shape[0], mem.shape[0]
    return pl.pallas_call(
        _kv_kernel,
        grid=(depth, batch),
        in_specs=[
            pl.BlockSpec((1, N_MEM, D_MODEL), lambda l, b: (b, 0, 0)),
            pl.BlockSpec((1, 1, D_MODEL), lambda l, b: (l, 0, 0)),
            pl.BlockSpec((1, D_MODEL, 2 * XW), lambda l, b: (l, 0, 0)),
        ],
        out_specs=[
            pl.BlockSpec((1, 1, XW, HEADS * N_MEM), lambda l, b: (l, b, 0, 0)),
            pl.BlockSpec((1, 1, HEADS * N_MEM, 2 * XW), lambda l, b: (l, b, 0, 0)),
        ],
        out_shape=[
            jax.ShapeDtypeStruct((depth, batch, XW, HEADS * N_MEM), _BF16),
            jax.ShapeDtypeStruct((depth, batch, HEADS * N_MEM, 2 * XW), _BF16),
        ],
        name="kv_proj",
    )(mem, mem_norm.reshape(depth, 1, D_MODEL), w_kv)


def _ffn_kernel(x_ref, g_ref, wgu_ref, wd_ref, fin_ref, *refs, final_norm, n_cast):
    o_ref, (h_ref, act_ref) = refs[n_cast], refs[2 * n_cast + 1:]
    half = FFN_STEP_ROWS // 2
    halves = (pl.ds(0, half), pl.ds(half, half))
    for rows in halves:
        h_ref[rows, :] = _rms_norm(x_ref[rows, :], g_ref[0]).astype(_BF16)
    for ci, (c0, cw) in enumerate(FF_CHUNKS):
        for rows in (halves if ci == 0 else (pl.ds(0, FFN_STEP_ROWS),)):
            gate = _dot(h_ref[rows, :], wgu_ref[0, :, c0:c0 + cw])
            up =_dot(h_ref[rows, :], wgu_ref[0, :, D_FF + c0:D_FF + c0 + cw])
            act_ref[rows, c0:c0 + cw] = (gate * jax.nn.sigmoid(gate) * up).astype(_BF16)
    y = x_ref[...] + 0.5 * _dot(act_ref[...], wd_ref[0])
    if final_norm:
        y = _rms_norm(y, fin_ref[...])
    o_ref[...] = y
    _cast_slices(refs, n_cast)


def _ffn_call(x, g, g_layer, wgu, wd, w_layer, fin, *, final_norm, casts=()):
    t = x.shape[0]
    n_steps = t // FFN_STEP_ROWS
    row = pl.BlockSpec((FFN_STEP_ROWS, D_MODEL), lambda i: (i, 0))
    jobs = [_cast_job(w, li, n_steps) for w, li in casts]
    outs = pl.pallas_call(
        functools.partial(_ffn_kernel, final_norm=final_norm, n_cast=len(casts)),
        grid=(n_steps,),
        in_specs=[
            row,
            _layer_resident(g.shape, g_layer),
            _layer_resident(wgu.shape, w_layer),
            _layer_resident(wd.shape, w_layer),
            pl.BlockSpec(fin.shape, lambda i: (0, 0), pipeline_mode=pl.Buffered(1)),
        ] + [j[0] for j in jobs],
        out_specs=[row] + [j[1] for j in jobs],
        out_shape=[jax.ShapeDtypeStruct((t, D_MODEL), _F32)] + [j[2] for j in jobs],
        scratch_shapes=[pltpu.VMEM((FFN_STEP_ROWS, D_MODEL), _BF16),
                        pltpu.VMEM((FFN_STEP_ROWS, D_FF), _BF16)],
        compiler_params=pltpu.CompilerParams(
            dimension_semantics=("arbitrary",), vmem_limit_bytes=VMEM_LIMIT_BYTES),
        name="ffn",
    )(x, g, wgu, wd, fin, *[w for w, _ in casts])
    return outs[0], outs[1:]


def _row_specs(total_rows):
    last_tile = total_rows // ROW_TILE - 1
    step = pl.BlockSpec((STEP_ROWS, D_MODEL), lambda i: (i, 0))
    nxt = pl.BlockSpec((ROW_TILE, D_MODEL),
                       lambda i: (jnp.minimum((i + 1) * SUB_TILES, last_tile), 0))
    return step, nxt


def _zero_bits_of(arrays):
    parts = []
    for a in arrays:
        u = pltpu.bitcast(a, jnp.uint32)
        parts += [u[r:r + 8, c:c + LANES]
                  for r in range(0, u.shape[0], 8) for c in range(0, u.shape[1], LANES)]
    while len(parts) > 1:
        parts = [a | b for a, b in zip(parts[0::2], parts[1::2])] + parts[len(parts) & ~1:]
    half = jnp.uint32(16)
    return lax.shift_right_logical(lax.shift_right_logical(parts[0], half), half)


def _after(x, zero_bits):
    if zero_bits is None:
        return x
    zero = lax.bitcast_convert_type(zero_bits[0:1, :], jnp.int32).astype(_F32)
    return x + jnp.concatenate([zero] * (x.shape[1] // LANES), axis=1)


def _cast_slices(refs, n_cast):
    done = []
    for src, dst in zip(refs[:n_cast], refs[n_cast + 1:2 * n_cast + 1]):
        done.append(src[0].astype(_BF16))
        dst[0] = done[-1]
    return done


def _for_each_tile(x_ref, xn_ref, g, h_ref, carry_ref, tile_fn, side_work):
    def norm_into(slot, x):
        h = _rms_norm(x, g).astype(_BF16)
        h_ref[slot] = h
        return h

    @pl.when(pl.program_id(0) == 0)
    def _():
        norm_into(0, x_ref[0:ROW_TILE, :])
        carry_ref[...] = jnp.zeros_like(carry_ref)

    for s in range(SUB_TILES):
        zero_bits = None
        if s + 1 < SUB_TILES:
            norm_into((s + 1) % 2, x_ref[(s + 1) * ROW_TILE:(s + 2) * ROW_TILE, :])
        else:
            zero_bits = _zero_bits_of([norm_into((s + 1) % 2, xn_ref[...])])
        tile_fn(s, pl.ds(s * ROW_TILE, ROW_TILE), s % 2, zero_bits)
    side_work()


def _cross_attention(q, kbd_ref, vbd_ref):
    s = _dot(q.astype(_BF16), kbd_ref[0, 0])
    es = []
    for hd in range(HEADS):
        sh = s[:, hd * N_MEM:(hd + 1) * N_MEM]
        es.append(jnp.exp(sh - jnp.max(sh, axis=-1, keepdims=True)).astype(_BF16))
    pv = _dot(jnp.concatenate(es, axis=1), vbd_ref[0, 0])
    return pv[:, :XW] / pv[:, XW:]


def _tile_in_batch(s, tiles_per_batch):
    return (pl.program_id(0) * SUB_TILES + s) % tiles_per_batch


def _carry_in(carry_ref, tile_in_batch):
    return jnp.where(tile_in_batch == 0, 0.0, carry_ref[...])


def _conv_mixer_kernel(x_ref, xn_ref, g_ref, win_ref, cw_ref, kbd_ref, vbd_ref, wout_ref,
                       *refs, tiles_per_batch, n_cast):
    o_ref, (h_ref, carry_ref) = refs[n_cast], refs[2 * n_cast + 1:]
    def tile(s, rows, slot, zero_bits):
        carry = _carry_in(carry_ref, _tile_in_batch(s, tiles_per_batch))
        z = _dot(h_ref[slot], win_ref[0])
        gate_b = z[:, :MW]
        cv = z[:, MW:2 * MW] * z[:, 2 * MW:3 * MW]
        q = z[:, 3 * MW:]
        row = lax.broadcasted_iota(jnp.int32, (ROW_TILE, 1), 0)
        prev1 = carry[CARRY_ROWS - 1:CARRY_ROWS, :]
        prev2 = carry[CARRY_ROWS - 2:CARRY_ROWS - 1, :]
        cv1 = jnp.where(row == 0, prev1, pltpu.roll(cv, 1, axis=0))
        cv2 = jnp.where(row == 0, prev2,
                        jnp.where(row == 1, prev1, pltpu.roll(cv, 2, axis=0)))
        carry_ref[...] = cv[ROW_TILE - CARRY_ROWS:, :]
        mix = gate_b * (cw_ref[0, 0:1, :] * cv2 + cw_ref[0, 1:2, :] * cv1
                        + cw_ref[0, 2:3, :] * cv)
        att = _cross_attention(_after(q, zero_bits), kbd_ref, vbd_ref)
        cat = jnp.concatenate([mix.astype(_BF16), att.astype(_BF16)], axis=1)
        o_ref[rows, :] = x_ref[rows, :] + _dot(cat, wout_ref[0])

    _for_each_tile(x_ref, xn_ref, g_ref[0], h_ref, carry_ref, tile,
                   functools.partial(_cast_slices, refs, n_cast))


def _pool_mixer_kernel(x_ref, xn_ref, g_ref, win_ref, wgrp_ref, sc_ref, kbd_ref, vbd_ref,
                       wout_ref, *refs, tiles_per_batch, n_cast):
    o_ref, (h_ref, carry_ref) = refs[n_cast], refs[2 * n_cast + 1:]
    def tile(s, rows, slot, zero_bits):
        tile_in_batch = _tile_in_batch(s, tiles_per_batch)
        carry = _carry_in(carry_ref, tile_in_batch)
        z = _dot(h_ref[slot], win_ref[0])
        p = z[:, :MW]
        q = z[:, MW:]
        ext = jnp.concatenate([carry, p], axis=0)
        carry_ref[...] = p[ROW_TILE - CARRY_ROWS:, :]
        lane_group = lax.broadcasted_iota(jnp.int32, (1, MW), 1) // POOL_GROUP_DIM
        inv_win = jnp.zeros((1, MW), _F32)
        levels = []
        acc, base, span = ext, 0, 1
        for gi, win in enumerate(POOL_WINDOWS):
            first = gi * POOL_GROUP_DIM // LANES * LANES
            acc, base = acc[:, first - base:], first
            while span < win:
                acc = acc + pltpu.roll(acc, span, axis=0)
                span *= 2
            levels.append((acc, base))
            inv_win = jnp.where(lane_group == gi, 1.0 / win, inv_win)
        cols = []
        for c0 in range(0, MW, LANES):
            col = None
            for gi in range(c0 // POOL_GROUP_DIM, (c0 + LANES - 1) // POOL_GROUP_DIM + 1):
                sums, first = levels[gi]
                part = sums[CARRY_ROWS:, c0 - first:c0 - first + LANES]
                col = part if col is None else jnp.where(
                    lane_group[:, c0:c0 + LANES] == gi, part, col)
            cols.append(col)
        wsum = jnp.concatenate(cols, axis=1)
        pos = tile_in_batch * ROW_TILE + lax.broadcasted_iota(jnp.int32, (ROW_TILE, 1), 0)
        inv_count = jnp.maximum(1.0 / (pos.astype(_F32) + 1.0), inv_win)
        d = (wsum * inv_count - p).astype(_BF16)
        mix = _dot(d, wgrp_ref[0]) * sc_ref[0]
        att = _cross_attention(_after(q, zero_bits), kbd_ref, vbd_ref)
        cat = jnp.concatenate([mix.astype(_BF16), att.astype(_BF16)], axis=1)
        o_ref[rows, :] = x_ref[rows, :] + _dot(cat, wout_ref[0])

    _for_each_tile(x_ref, xn_ref, g_ref[0], h_ref, carry_ref, tile,
                   functools.partial(_cast_slices, refs, n_cast))


def _cast_job(w, layer, n_steps):
    _, r, c = w.shape
    n_blocks = n_steps
    while r % n_blocks or (r // n_blocks) % BF16_SUBLANES:
        n_blocks //= 2
    block = (1, r // n_blocks, c)
    src = pl.BlockSpec(block, lambda i: (layer, i * n_blocks // n_steps, 0))
    dst = pl.BlockSpec(block, lambda i: (0, i * n_blocks // n_steps, 0))
    return src, dst, jax.ShapeDtypeStruct((1, r, c), _BF16)


def _mixer_call(kernel_fn, name, x, layer, kbd, vbd, weights_before, w_out, seq, casts):
    t = x.shape[0]
    n_steps = t // STEP_ROWS
    tiles_per_batch = seq // ROW_TILE
    steps_per_batch = seq // STEP_ROWS
    step, nxt = _row_specs(t)
    kbd_spec = pl.BlockSpec((1, 1, XW, HEADS * N_MEM),
                            lambda i: (layer, i // steps_per_batch, 0, 0))
    vbd_spec = pl.BlockSpec((1, 1, HEADS * N_MEM, 2 * XW),
                             lambda i: (layer, i // steps_per_batch, 0, 0))
    jobs = [_cast_job(w, li, n_steps) for w, li in casts]
    outs = pl.pallas_call(
        functools.partial(kernel_fn, tiles_per_batch=tiles_per_batch, n_cast=len(casts)),
        grid=(n_steps,),
        in_specs=[step, nxt] + [_layer_resident(w.shape, li) for w, li in weights_before]
        + [kbd_spec, vbd_spec, _layer_resident(w_out.shape, 0)] + [j[0] for j in jobs],
        out_specs=[step] + [j[1] for j in jobs],
        out_shape=[jax.ShapeDtypeStruct((t, D_MODEL), _F32)] + [j[2] for j in jobs],
        scratch_shapes=[pltpu.VMEM((2, ROW_TILE, D_MODEL), _BF16),
                        pltpu.VMEM((CARRY_ROWS, MW), _F32)],
        compiler_params=pltpu.CompilerParams(
            dimension_semantics=("arbitrary",), vmem_limit_bytes=VMEM_LIMIT_BYTES),
        name=name,
    )(x, x, *[w for w, _ in weights_before], kbd, vbd, w_out, *[w for w, _ in casts])
    return outs[0], outs[1:]


def _block_diag(w_group):
    nl, g, d, _ = w_group.shape
    out = jnp.zeros((nl, g * d, g * d), w_group.dtype)
    for gi in range(g):
        out = lax.dynamic_update_slice(out, w_group[:, gi], (0, gi * d, gi * d))
    return out


def kernel(x, mem, ffn1_norm, ffn1_w_gu, ffn1_w_down, mix_norm, mem_norm, w_kv, w_out,
           conv_w_in, conv_w, pool_w_in, pool_w_group, pool_scale,
           ffn2_norm, ffn2_w_gu, ffn2_w_down, final_norm):
    batch, seq, _ = x.shape
    depth = ffn1_norm.shape[0]
    assert seq % STEP_ROWS == 0 and SUB_TILES % 2 == 0 and (batch * seq) % FFN_STEP_ROWS == 0
    assert ROW_TILE >= CARRY_ROWS >= max(POOL_WINDOWS) - 1

    bf = lambda w: w.astype(_BF16)
    vec = lambda v: v.reshape(v.shape[0], 1, v.shape[-1])
    kbd, vbd = _kv_call(mem, mem_norm, w_kv)
    fin = final_norm.reshape(1, D_MODEL)
    g1, g2, mix_g = vec(ffn1_norm), vec(ffn2_norm), vec(mix_norm)
    pool_grp, pool_sc = bf(_block_diag(pool_w_group)), vec(pool_scale)

    xt = x.reshape(batch * seq, D_MODEL)
    wgu, wd = bf(ffn1_w_gu[0:1]), bf(ffn1_w_down[0:1])
    mixer_casts = []
    ia = ib = 0
    for i in range(depth):
        mixer_casts += [(w_out, i), (conv_w_in, ia) if i % 2 == 0 else (pool_w_in, ib)]
        ia, ib = ia + (i % 2 == 0), ib + (i % 2 == 1)
    mixer_w = None
    ia = ib = 0
    for i in range(depth):
        xt, cast = _ffn_call(xt, g1, i, wgu, wd, 0, fin, final_norm=False,
                             casts=mixer_casts if i == 0 else ())
        mixer_w = cast if i == 0 else mixer_w
        w_out_i, w_in_i = mixer_w[2 * i], mixer_w[2 * i + 1]
        casts = [(ffn2_w_gu, i), (ffn2_w_down, i)]
        if i + 1 < depth:
            casts += [(ffn1_w_gu, i + 1), (ffn1_w_down, i + 1)]
        if i % 2 == 0:
            xt, cast = _mixer_call(_conv_mixer_kernel, "conv_mixer", xt, i, kbd, vbd,
                                   [(mix_g, i), (w_in_i, 0), (conv_w, ia)], w_out_i, seq,
                                   casts)
            ia += 1
        else:
            xt, cast = _mixer_call(_pool_mixer_kernel, "pool_mixer", xt, i, kbd, vbd,
                                   [(mix_g, i), (w_in_i, 0), (pool_grp, ib), (pool_sc, ib)],
                                   w_out_i, seq, casts)
            ib += 1
        xt, _ = _ffn_call(xt, g2, i, cast[0], cast[1], 0, fin, final_norm=(i == depth - 1))
        if i + 1 < depth:
            wgu, wd = cast[2], cast[3]
    return xt.reshape(batch, seq, D_MODEL)
```

```python
import functools

import jax
import jax.numpy as jnp
from jax import lax
from jax.experimental import pallas as pl
from jax.experimental.pallas import tpu as pltpu

D_MODEL = 1024
N_MEM = 256
HEADS = 4
HEAD_DIM = 64
XW = HEADS * HEAD_DIM
MW = D_MODEL - XW
CONV_WIDTH = 3
POOL_WINDOWS = (2, 4, 8, 16)
POOL_GROUP_DIM = MW // len(POOL_WINDOWS)
D_FF = 2816
RMS_EPS = 1e-6

STEP_ROWS = 1024
FF_CHUNKS = ((0, 512), (512, 512), (1024, 512), (1536, 512), (2048, 512), (2560, 256))
CARRY_ROWS = 16
BF16_SUBLANES = 16
LANES = 128
VMEM_LIMIT_BYTES = 56 * 1024 * 1024

_BF16 = jnp.bfloat16
_F32 = jnp.float32


def _rms_norm(x, g):
    y = x * lax.rsqrt(jnp.mean(x * x, axis=-1, keepdims=True) + RMS_EPS)
    return y * g


def _dot(a, b):
    return jnp.dot(a, b, preferred_element_type=_F32)


def _layer_resident(shape, layer):
    nd = len(shape) - 1
    return pl.BlockSpec((1,) + tuple(shape[1:]), lambda *_: (layer,) + (0,) * nd,
                        pipeline_mode=pl.Buffered(1))


def _norm_halves(x_ref, g, h_ref):
    half = STEP_ROWS // 2
    halves = (pl.ds(0, half), pl.ds(half, half))
    for rows in halves:
        h_ref[rows, :] = _rms_norm(x_ref[rows, :], g).astype(_BF16)
    return halves


def _cast_job(w, layer, n_steps, step_of=lambda i: i):
    _, r, c = w.shape
    n_blocks = n_steps
    while r % n_blocks or (r // n_blocks) % BF16_SUBLANES:
        n_blocks //= 2
    block = (1, r // n_blocks, c)
    src = pl.BlockSpec(block, lambda *i: (layer, step_of(*i) * n_blocks // n_steps, 0))
    dst = pl.BlockSpec(block, lambda *i: (0, step_of(*i) * n_blocks // n_steps, 0))
    return src, dst, jax.ShapeDtypeStruct((1, r, c), _BF16)


def _cast_slices(srcs, dsts):
    for src, dst in zip(srcs, dsts):
        dst[...] = src[...].astype(_BF16)


def _kv_kernel(mem_ref, g_ref, wkv_ref, *refs, n_cast):
    kbd_ref, vbd_ref = refs[n_cast:n_cast + 2]
    mem_h = _rms_norm(mem_ref[0], g_ref[0])
    kv = _dot(mem_h, wkv_ref[0])
    k = kv[:, :XW] * (HEAD_DIM ** -0.5)
    v = kv[:, XW:]
    kt = jnp.concatenate([k.T] * HEADS, axis=1)
    r = lax.broadcasted_iota(jnp.int32, kt.shape, 0) // HEAD_DIM
    c = lax.broadcasted_iota(jnp.int32, kt.shape, 1) // N_MEM
    kbd_ref[0, 0] = jnp.where(r == c, kt, 0.0).astype(_BF16)
    vt = jnp.concatenate([v] * HEADS, axis=0)
    r = lax.broadcasted_iota(jnp.int32, vt.shape, 0) // N_MEM
    c = lax.broadcasted_iota(jnp.int32, vt.shape, 1) // HEAD_DIM
    blk = r == c
    vbd_ref[0, 0, :, :XW] = jnp.where(blk, vt, 0.0).astype(_BF16)
    vbd_ref[0, 0, :, XW:] = jnp.where(blk, 1.0, 0.0).astype(_BF16)
    _cast_slices(refs[:n_cast], refs[n_cast + 2:])


def _kv_call(mem, mem_norm, w_kv, casts):
    depth, batch = mem_norm.shape[0], mem.shape[0]
    jobs = [_cast_job(w, li, depth * batch, lambda l, b: l * batch + b) for w, li in casts]
    outs = pl.pallas_call(
        functools.partial(_kv_kernel, n_cast=len(casts)),
        grid=(depth, batch),
        in_specs=[
            pl.BlockSpec((1, N_MEM, D_MODEL), lambda l, b: (b, 0, 0)),
            pl.BlockSpec((1, 1, D_MODEL), lambda l, b: (l, 0, 0)),
            pl.BlockSpec((1, D_MODEL, 2 * XW), lambda l, b: (l, 0, 0)),
        ] + [j[0] for j in jobs],
        out_specs=[
            pl.BlockSpec((1, 1, XW, HEADS * N_MEM), lambda l, b: (l, b, 0, 0)),
            pl.BlockSpec((1, 1, HEADS * N_MEM, 2 * XW), lambda l, b: (l, b, 0, 0)),
        ] + [j[1] for j in jobs],
        out_shape=[
            jax.ShapeDtypeStruct((depth, batch, XW, HEADS * N_MEM), _BF16),
            jax.ShapeDtypeStruct((depth, batch, HEADS * N_MEM, 2 * XW), _BF16),
        ] + [j[2] for j in jobs],
        compiler_params=pltpu.CompilerParams(
            dimension_semantics=("arbitrary", "arbitrary"), vmem_limit_bytes=VMEM_LIMIT_BYTES),
        name="kv_proj",
    )(mem, mem_norm.reshape(depth, 1, D_MODEL), w_kv, *[w for w, _ in casts])
    return outs[0], outs[1], outs[2:]


def _ffn_kernel(x_ref, g_ref, wgu_ref, wd_ref, fin_ref, *refs, final_norm, n_cast):
    o_ref, (h_ref, act_ref) = refs[n_cast], refs[2 * n_cast + 1:]
    halves = _norm_halves(x_ref, g_ref[0], h_ref)
    for ci, (c0, cw) in enumerate(FF_CHUNKS):
        for rows in (halves if ci == 0 else (pl.ds(0, STEP_ROWS),)):
            gate = _dot(h_ref[rows, :], wgu_ref[0, :, c0:c0 + cw])
            up = _dot(h_ref[rows, :], wgu_ref[0, :, D_FF + c0:D_FF + c0 + cw])
            act_ref[rows, c0:c0 + cw] = (gate * jax.nn.sigmoid(gate) * up).astype(_BF16)
    y = x_ref[...] + 0.5 * _dot(act_ref[...], wd_ref[0])
    if final_norm:
        y = _rms_norm(y, fin_ref[...])
    o_ref[...] = y
    _cast_slices(refs[:n_cast], refs[n_cast + 1:2 * n_cast + 1])


def _ffn_call(x, g, g_layer, wgu, wd, fin, *, final_norm, casts=()):
    t = x.shape[0]
    n_steps = t // STEP_ROWS
    row = pl.BlockSpec((STEP_ROWS, D_MODEL), lambda i: (i, 0))
    jobs = [_cast_job(w, li, n_steps) for w, li in casts]
    outs = pl.pallas_call(
        functools.partial(_ffn_kernel, final_norm=final_norm, n_cast=len(casts)),
        grid=(n_steps,),
        in_specs=[
            row,
            _layer_resident(g.shape, g_layer),
            _layer_resident(wgu.shape, 0),
            _layer_resident(wd.shape, 0),
            pl.BlockSpec(fin.shape, lambda i: (0, 0), pipeline_mode=pl.Buffered(1)),
        ] + [j[0] for j in jobs],
        out_specs=[row] + [j[1] for j in jobs],
        out_shape=[jax.ShapeDtypeStruct((t, D_MODEL), _F32)] + [j[2] for j in jobs],
        scratch_shapes=[pltpu.VMEM((STEP_ROWS, D_MODEL), _BF16),
                        pltpu.VMEM((STEP_ROWS, D_FF), _BF16)],
        compiler_params=pltpu.CompilerParams(
            dimension_semantics=("arbitrary",), vmem_limit_bytes=VMEM_LIMIT_BYTES),
        name="ffn",
    )(x, g, wgu, wd, fin, *[w for w, _ in casts])
    return outs[0], outs[1:]


def _cross_attention(q, kbd_ref, vbd_ref):
    s = _dot(q.astype(_BF16), kbd_ref[0, 0])
    es = []
    for hd in range(HEADS):
        sh = s[:, hd * N_MEM:(hd + 1) * N_MEM]
        es.append(jnp.exp(sh - jnp.max(sh, axis=-1, keepdims=True)).astype(_BF16))
    pv = _dot(jnp.concatenate(es, axis=1), vbd_ref[0, 0])
    return pv[:, :XW] / pv[:, XW:]


def _carry_in(carry_ref, step_in_batch):
    @pl.when(pl.program_id(0) == 0)
    def _():
        carry_ref[...] = jnp.zeros_like(carry_ref)

    return jnp.where(step_in_batch == 0, 0.0, carry_ref[...])


def _conv_mixer_kernel(x_ref, g_ref, win_ref, cw_ref, kbd_ref, vbd_ref, wout_ref, *refs,
                       steps_per_batch, n_cast):
    o_ref, (h_ref, carry_ref) = refs[n_cast], refs[2 * n_cast + 1:]
    carry = _carry_in(carry_ref, pl.program_id(0) % steps_per_batch)
    _norm_halves(x_ref, g_ref[0], h_ref)
    q = _dot(h_ref[...], win_ref[0, :, 3 * MW:])
    gcv = _dot(h_ref[...], win_ref[0, :, MW:3 * MW])
    cv = gcv[:, :MW] * gcv[:, MW:]
    gate_b = _dot(h_ref[...], win_ref[0, :, :MW])
    row = lax.broadcasted_iota(jnp.int32, (STEP_ROWS, 1), 0)
    prev1 = carry[CARRY_ROWS - 1:CARRY_ROWS, :]
    prev2 = carry[CARRY_ROWS - 2:CARRY_ROWS - 1, :]
    cv1 = jnp.where(row == 0, prev1, pltpu.roll(cv, 1, axis=0))
    cv2 = jnp.where(row == 0, prev2, jnp.where(row == 1, prev1, pltpu.roll(cv, 2, axis=0)))
    carry_ref[...] = cv[STEP_ROWS - CARRY_ROWS:, :]
    mix = gate_b * (cw_ref[0, 0:1, :] * cv2 + cw_ref[0, 1:2, :] * cv1 + cw_ref[0, 2:3, :] * cv)
    att = _cross_attention(q, kbd_ref, vbd_ref)
    cat = jnp.concatenate([mix.astype(_BF16), att.astype(_BF16)], axis=1)
    o_ref[...] = x_ref[...] + _dot(cat, wout_ref[0])
    _cast_slices(refs[:n_cast], refs[n_cast + 1:2 * n_cast + 1])


def _pool_mixer_kernel(x_ref, g_ref, win_ref, wgrp_ref, sc_ref, kbd_ref, vbd_ref, wout_ref,
                       *refs, steps_per_batch, n_cast):
    o_ref, (h_ref, carry_ref) = refs[n_cast], refs[2 * n_cast + 1:]
    step_in_batch = pl.program_id(0) % steps_per_batch
    carry = _carry_in(carry_ref, step_in_batch)
    _norm_halves(x_ref, g_ref[0], h_ref)
    z = _dot(h_ref[...], win_ref[0])
    p = z[:, :MW]
    q = z[:, MW:]
    ext = jnp.concatenate([carry, p], axis=0)
    carry_ref[...] = p[STEP_ROWS - CARRY_ROWS:, :]
    lane_group = lax.broadcasted_iota(jnp.int32, (1, MW), 1) // POOL_GROUP_DIM
    inv_win = jnp.zeros((1, MW), _F32)
    levels = []
    acc, base, span = ext, 0, 1
    for gi, win in enumerate(POOL_WINDOWS):
        first = gi * POOL_GROUP_DIM // LANES * LANES
        acc, base = acc[:, first - base:], first
        while span < win:
            acc = acc + pltpu.roll(acc, span, axis=0)
            span *= 2
        levels.append((acc, base))
        inv_win = jnp.where(lane_group == gi, 1.0 / win, inv_win)
    cols = []
    for c0 in range(0, MW, LANES):
        col = None
        for gi in range(c0 // POOL_GROUP_DIM, (c0 + LANES - 1) // POOL_GROUP_DIM + 1):
            sums, first = levels[gi]
            part = sums[CARRY_ROWS:, c0 - first:c0 - first + LANES]
            col = part if col is None else jnp.where(
                lane_group[:, c0:c0 + LANES] == gi, part, col)
        cols.append(col)
    wsum = jnp.concatenate(cols, axis=1)
    pos = step_in_batch * STEP_ROWS + lax.broadcasted_iota(jnp.int32, (STEP_ROWS, 1), 0)
    inv_count = jnp.maximum(1.0 / (pos.astype(_F32) + 1.0), inv_win)
    d = (wsum * inv_count - p).astype(_BF16)
    mix = _dot(d, wgrp_ref[0]) * sc_ref[0]
    att = _cross_attention(q, kbd_ref, vbd_ref)
    cat = jnp.concatenate([mix.astype(_BF16), att.astype(_BF16)], axis=1)
    o_ref[...] = x_ref[...] + _dot(cat, wout_ref[0])
    _cast_slices(refs[:n_cast], refs[n_cast + 1:2 * n_cast + 1])


def _mixer_call(kernel_fn, name, x, layer, kbd, vbd, weights_before, w_out, seq, casts):
    t = x.shape[0]
    n_steps = t // STEP_ROWS
    steps_per_batch = seq // STEP_ROWS
    row = pl.BlockSpec((STEP_ROWS, D_MODEL), lambda i: (i, 0))
    kbd_spec = pl.BlockSpec((1, 1, XW, HEADS * N_MEM),
                            lambda i: (layer, i // steps_per_batch, 0, 0))
    vbd_spec = pl.BlockSpec((1, 1, HEADS * N_MEM, 2 * XW),
                            lambda i: (layer, i // steps_per_batch, 0, 0))
    jobs = [_cast_job(w, li, n_steps) for w, li in casts]
    outs = pl.pallas_call(
        functools.partial(kernel_fn, steps_per_batch=steps_per_batch, n_cast=len(casts)),
        grid=(n_steps,),
        in_specs=[row] + [_layer_resident(w.shape, li) for w, li in weights_before]
        + [kbd_spec, vbd_spec, _layer_resident(w_out.shape, 0)] + [j[0] for j in jobs],
        out_specs=[row] + [j[1] for j in jobs],
        out_shape=[jax.ShapeDtypeStruct((t, D_MODEL), _F32)] + [j[2] for j in jobs],
        scratch_shapes=[pltpu.VMEM((STEP_ROWS, D_MODEL), _BF16),
                        pltpu.VMEM((CARRY_ROWS, MW), _F32)],
        compiler_params=pltpu.CompilerParams(
            dimension_semantics=("arbitrary",), vmem_limit_bytes=VMEM_LIMIT_BYTES),
        name=name,
    )(x, *[w for w, _ in weights_before], kbd, vbd, w_out, *[w for w, _ in casts])
    return outs[0], outs[1:]


def _block_diag(w_group):
    nl, g, d, _ = w_group.shape
    out = jnp.zeros((nl, g * d, g * d), w_group.dtype)
    for gi in range(g):
        out = lax.dynamic_update_slice(out, w_group[:, gi], (0, gi * d, gi * d))
    return out


def kernel(x, mem, ffn1_norm, ffn1_w_gu, ffn1_w_down, mix_norm, mem_norm, w_kv, w_out,
           conv_w_in, conv_w, pool_w_in, pool_w_group, pool_scale,
           ffn2_norm, ffn2_w_gu, ffn2_w_down, final_norm):
    batch, seq, _ = x.shape
    depth = ffn1_norm.shape[0]
    assert seq % STEP_ROWS == 0 and STEP_ROWS >= CARRY_ROWS >= max(POOL_WINDOWS) - 1

    vec = lambda v: v.reshape(v.shape[0], 1, v.shape[-1])
    fin = final_norm.reshape(1, D_MODEL)
    g1, g2, mix_g = vec(ffn1_norm), vec(ffn2_norm), vec(mix_norm)
    pool_grp, pool_sc = _block_diag(pool_w_group).astype(_BF16), vec(pool_scale)

    kbd, vbd, (wgu, wd) = _kv_call(mem, mem_norm, w_kv, [(ffn1_w_gu, 0), (ffn1_w_down, 0)])
    mixer_casts = []
    ia = ib = 0
    for i in range(depth):
        mixer_casts += [(w_out, i), (conv_w_in, ia) if i % 2 == 0 else (pool_w_in, ib)]
        ia, ib = ia + (i % 2 == 0), ib + (i % 2 == 1)

    xt = x.reshape(batch * seq, D_MODEL)
    mixer_w = None
    ia = ib = 0
    for i in range(depth):
        xt, cast = _ffn_call(xt, g1, i, wgu, wd, fin, final_norm=False,
                             casts=mixer_casts if i == 0 else ())
        mixer_w = cast if i == 0 else mixer_w
        w_out_i, w_in_i = mixer_w[2 * i], mixer_w[2 * i + 1]
        casts = [(ffn2_w_gu, i), (ffn2_w_down, i)]
        if i + 1 < depth:
            casts += [(ffn1_w_gu, i + 1), (ffn1_w_down, i + 1)]
        if i % 2 == 0:
            xt, cast = _mixer_call(_conv_mixer_kernel, "conv_mixer", xt, i, kbd, vbd,
                                   [(mix_g, i), (w_in_i, 0), (conv_w, ia)], w_out_i, seq,
                                   casts)
            ia += 1
        else:
            xt, cast = _mixer_call(_pool_mixer_kernel, "pool_mixer", xt, i, kbd, vbd,
                                   [(mix_g, i), (w_in_i, 0), (pool_grp, ib), (pool_sc, ib)],
                                   w_out_i, seq, casts)
            ib += 1
        xt, _ = _ffn_call(xt, g2, i, cast[0], cast[1], fin, final_norm=(i == depth - 1))
        if i + 1 < depth:
            wgu, wd = cast[2], cast[3]
    return xt.reshape(batch, seq, D_MODEL)
```

```python
import functools

import jax
import jax.numpy as jnp
from jax import lax
from jax.experimental import pallas as pl
from jax.experimental.pallas import tpu as pltpu

D_MODEL = 1024
N_MEM = 256
HEADS = 4
HEAD_DIM = 64
XW = HEADS * HEAD_DIM
PAIR_W = 2 * HEAD_DIM
MW = D_MODEL - XW
CONV_WIDTH = 3
POOL_WINDOWS = (2, 4, 8, 16)
POOL_GROUP_DIM = MW // len(POOL_WINDOWS)
D_FF = 2816
RMS_EPS = 1e-6

STEP_ROWS = 1024
FF_CHUNKS = ((0, 512), (512, 512), (1024, 512), (1536, 512), (2048, 512), (2560, 256))
CARRY_ROWS = 16
BF16_SUBLANES = 16
LANES = 128
VMEM_LIMIT_BYTES = 56 * 1024 * 1024

_BF16 = jnp.bfloat16
_F32 = jnp.float32


def _rms_norm(x, g):
    y = x * lax.rsqrt(jnp.mean(x * x, axis=-1, keepdims=True) + RMS_EPS)
    return y * g


def _dot(a, b):
    return jnp.dot(a, b, preferred_element_type=_F32)


def _layer_resident(shape, layer):
    nd = len(shape) - 1
    return pl.BlockSpec((1,) + tuple(shape[1:]), lambda *_: (layer,) + (0,) * nd,
                        pipeline_mode=pl.Buffered(1))


def _norm_halves(x_ref, g, h_ref):
    half = STEP_ROWS // 2
    halves = (pl.ds(0, half // 2), pl.ds(half // 2, half // 2), pl.ds(half, half))
    for rows in halves:
        h_ref[rows, :] = _rms_norm(x_ref[rows, :], g).astype(_BF16)
    return halves


def _cast_job(w, layer, n_steps, step_of=lambda i: i):
    _, r, c = w.shape
    n_blocks = n_steps
    while r % n_blocks or (r // n_blocks) % BF16_SUBLANES:
        n_blocks //= 2
    block = (1, r // n_blocks, c)
    src = pl.BlockSpec(block, lambda *i: (layer, step_of(*i) * n_blocks // n_steps, 0))
    dst = pl.BlockSpec(block, lambda *i: (0, step_of(*i) * n_blocks // n_steps, 0))
    return src, dst, jax.ShapeDtypeStruct((1, r, c), _BF16)


def _cast_slices(srcs, dsts):
    for src, dst in zip(srcs, dsts):
        dst[...] = src[...].astype(_BF16)


def _kv_kernel(mem_ref, g_ref, wkv_ref, *refs, n_cast):
    kbd_ref, vbd_ref = refs[n_cast:n_cast + 2]
    mem_h = _rms_norm(mem_ref[0], g_ref[0])
    kv = _dot(mem_h, wkv_ref[0])
    k = kv[:, :XW] * (HEAD_DIM ** -0.5)
    v = kv[:, XW:]
    kt = jnp.concatenate([k.T] * HEADS, axis=1)
    r = lax.broadcasted_iota(jnp.int32, kt.shape, 0) // HEAD_DIM
    c = lax.broadcasted_iota(jnp.int32, kt.shape, 1) // N_MEM
    kbd_ref[0, 0] = jnp.where(r == c, kt, 0.0).astype(_BF16)
    for pr in range(HEADS // 2):
        vp = v[:, pr * PAIR_W:(pr + 1) * PAIR_W]
        vp2 = jnp.concatenate([vp, vp], axis=0)
        r = lax.broadcasted_iota(jnp.int32, vp2.shape, 0) // N_MEM
        c = lax.broadcasted_iota(jnp.int32, vp2.shape, 1) // HEAD_DIM
        own = r == c
        rows = pl.ds(pr * 2 * N_MEM, 2 * N_MEM)
        vbd_ref[0, 0, rows, :PAIR_W] = jnp.where(own, vp2, 0.0).astype(_BF16)
        vbd_ref[0, 0, rows, PAIR_W:] = jnp.where(own, 1.0, 0.0).astype(_BF16)
    _cast_slices(refs[:n_cast], refs[n_cast + 2:])


def _kv_call(mem, mem_norm, w_kv, casts):
    depth, batch = mem_norm.shape[0], mem.shape[0]
    jobs = [_cast_job(w, li, depth * batch, lambda l, b: l * batch + b) for w, li in casts]
    outs = pl.pallas_call(
        functools.partial(_kv_kernel, n_cast=len(casts)),
        grid=(depth, batch),
        in_specs=[
            pl.BlockSpec((1, N_MEM, D_MODEL), lambda l, b: (b, 0, 0)),
            pl.BlockSpec((1, 1, D_MODEL), lambda l, b: (l, 0, 0)),
            pl.BlockSpec((1, D_MODEL, 2 * XW), lambda l, b: (l, 0, 0)),
        ] + [j[0] for j in jobs],
        out_specs=[
            pl.BlockSpec((1, 1, XW, HEADS * N_MEM), lambda l, b: (l, b, 0, 0)),
            pl.BlockSpec((1, 1, HEADS * N_MEM, 2 * PAIR_W), lambda l, b: (l, b, 0, 0)),
        ] + [j[1] for j in jobs],
        out_shape=[
            jax.ShapeDtypeStruct((depth, batch, XW, HEADS * N_MEM), _BF16),
            jax.ShapeDtypeStruct((depth, batch, HEADS * N_MEM, 2 * PAIR_W), _BF16),
        ] + [j[2] for j in jobs],
        compiler_params=pltpu.CompilerParams(
            dimension_semantics=("arbitrary", "arbitrary"), vmem_limit_bytes=VMEM_LIMIT_BYTES),
        name="kv_proj",
    )(mem, mem_norm.reshape(depth, 1, D_MODEL), w_kv, *[w for w, _ in casts])
    return outs[0], outs[1], outs[2:]


def _ffn_kernel(x_ref, g_ref, wgu_ref, wd_ref, fin_ref, *refs, final_norm, n_cast):
    o_ref, (h_ref, act_ref) = refs[n_cast], refs[2 * n_cast + 1:]
    halves = _norm_halves(x_ref, g_ref[0], h_ref)
    for ci, (c0, cw) in enumerate(FF_CHUNKS):
        for rows in (halves if ci == 0 else (pl.ds(0, STEP_ROWS),)):
            gate = _dot(h_ref[rows, :], wgu_ref[0, :, c0:c0 + cw])
            up = _dot(h_ref[rows, :], wgu_ref[0, :, D_FF + c0:D_FF + c0 + cw])
            act_ref[rows, c0:c0 + cw] = (gate * jax.nn.sigmoid(gate) * up).astype(_BF16)
    if final_norm:
        half = STEP_ROWS // 2
        for rows in (pl.ds(0, half), pl.ds(half, half)):
            y = x_ref[rows, :] + 0.5 * _dot(act_ref[rows, :], wd_ref[0])
            o_ref[rows, :] = _rms_norm(y, fin_ref[...])
    else:
        o_ref[...] = x_ref[...] + 0.5 * _dot(act_ref[...], wd_ref[0])
    _cast_slices(refs[:n_cast], refs[n_cast + 1:2 * n_cast + 1])


def _ffn_call(x, g, g_layer, wgu, wd, fin, *, final_norm, casts=()):
    t = x.shape[0]
    n_steps = t // STEP_ROWS
    row = pl.BlockSpec((STEP_ROWS, D_MODEL), lambda i: (i, 0))
    jobs = [_cast_job(w, li, n_steps) for w, li in casts]
    outs = pl.pallas_call(
        functools.partial(_ffn_kernel, final_norm=final_norm, n_cast=len(casts)),
        grid=(n_steps,),
        in_specs=[
            row,
            _layer_resident(g.shape, g_layer),
            _layer_resident(wgu.shape, 0),
            _layer_resident(wd.shape, 0),
            pl.BlockSpec(fin.shape, lambda i: (0, 0), pipeline_mode=pl.Buffered(1)),
        ] + [j[0] for j in jobs],
        out_specs=[row] + [j[1] for j in jobs],
        out_shape=[jax.ShapeDtypeStruct((t, D_MODEL), _F32)] + [j[2] for j in jobs],
        scratch_shapes=[pltpu.VMEM((STEP_ROWS, D_MODEL), _BF16),
                        pltpu.VMEM((STEP_ROWS, D_FF), _BF16)],
        compiler_params=pltpu.CompilerParams(
            dimension_semantics=("arbitrary",), vmem_limit_bytes=VMEM_LIMIT_BYTES),
        name="ffn",
    )(x, g, wgu, wd, fin, *[w for w, _ in casts])
    return outs[0], outs[1:]


def _cross_attention(q, kbd_ref, vbd_ref):
    s = _dot(q.astype(_BF16), kbd_ref[0, 0])
    es = []
    for hd in range(HEADS):
        sh = s[:, hd * N_MEM:(hd + 1) * N_MEM]
        es.append(jnp.exp(sh - jnp.max(sh, axis=-1, keepdims=True)).astype(_BF16))
    att = []
    for pr in range(HEADS // 2):
        pv = _dot(jnp.concatenate(es[2 * pr:2 * pr + 2], axis=1),
                  vbd_ref[0, 0, pl.ds(pr * 2 * N_MEM, 2 * N_MEM), :])
        att.append(pv[:, :PAIR_W] / pv[:, PAIR_W:])
    return jnp.concatenate(att, axis=1)


def _carry_in(carry_ref, step_in_batch):
    @pl.when(pl.program_id(0) == 0)
    def _():
        carry_ref[...] = jnp.zeros_like(carry_ref)

    return jnp.where(step_in_batch == 0, 0.0, carry_ref[...])


def _conv_mixer_kernel(x_ref, g_ref, win_ref, cw_ref, kbd_ref, vbd_ref, wout_ref, *refs,
                       steps_per_batch, n_cast):
    o_ref, (h_ref, carry_ref) = refs[n_cast], refs[2 * n_cast + 1:]
    carry = _carry_in(carry_ref, pl.program_id(0) % steps_per_batch)
    _cast_slices(refs[:n_cast], refs[n_cast + 1:2 * n_cast + 1])
    _norm_halves(x_ref, g_ref[0], h_ref)
    q = _dot(h_ref[...], win_ref[0, :, 3 * MW:])
    gcv = _dot(h_ref[...], win_ref[0, :, MW:3 * MW])
    cv = gcv[:, :MW] * gcv[:, MW:]
    gate_b = _dot(h_ref[...], win_ref[0, :, :MW])
    row = lax.broadcasted_iota(jnp.int32, (STEP_ROWS, 1), 0)
    prev1 = carry[CARRY_ROWS - 1:CARRY_ROWS, :]
    prev2 = carry[CARRY_ROWS - 2:CARRY_ROWS - 1, :]
    cv1 = jnp.where(row == 0, prev1, pltpu.roll(cv, 1, axis=0))
    cv2 = jnp.where(row == 0, prev2, jnp.where(row == 1, prev1, pltpu.roll(cv, 2, axis=0)))
    carry_ref[...] = cv[STEP_ROWS - CARRY_ROWS:, :]
    mix = gate_b * (cw_ref[0, 0:1, :] * cv2 + cw_ref[0, 1:2, :] * cv1 + cw_ref[0, 2:3, :] * cv)
    att = _cross_attention(q, kbd_ref, vbd_ref)
    cat = jnp.concatenate([mix.astype(_BF16), att.astype(_BF16)], axis=1)
    o_ref[...] = x_ref[...] + _dot(cat, wout_ref[0])


def _pool_mixer_kernel(x_ref, g_ref, win_ref, wgrp_ref, sc_ref, kbd_ref, vbd_ref, wout_ref,
                       *refs, steps_per_batch, n_cast):
    o_ref, (h_ref, carry_ref) = refs[n_cast], refs[2 * n_cast + 1:]
    step_in_batch = pl.program_id(0) % steps_per_batch
    carry = _carry_in(carry_ref, step_in_batch)
    _cast_slices(refs[:n_cast], refs[n_cast + 1:2 * n_cast + 1])
    _norm_halves(x_ref, g_ref[0], h_ref)
    z = _dot(h_ref[...], win_ref[0])
    p = z[:, :MW]
    q = z[:, MW:]
    ext = jnp.concatenate([carry, p], axis=0)
    carry_ref[...] = p[STEP_ROWS - CARRY_ROWS:, :]
    lane_group = lax.broadcasted_iota(jnp.int32, (1, MW), 1) // POOL_GROUP_DIM
    inv_win = jnp.zeros((1, MW), _F32)
    levels = []
    acc, base, span = ext, 0, 1
    for gi, win in enumerate(POOL_WINDOWS):
        first = gi * POOL_GROUP_DIM // LANES * LANES
        acc, base = acc[:, first - base:], first
        while span < win:
            acc = acc + pltpu.roll(acc, span, axis=0)
            span *= 2
        levels.append((acc, base))
        inv_win = jnp.where(lane_group == gi, 1.0 / win, inv_win)
    cols = []
    for c0 in range(0, MW, LANES):
        col = None
        for gi in range(c0 // POOL_GROUP_DIM, (c0 + LANES - 1) // POOL_GROUP_DIM + 1):
            sums, first = levels[gi]
            part = sums[CARRY_ROWS:, c0 - first:c0 - first + LANES]
            col = part if col is None else jnp.where(
                lane_group[:, c0:c0 + LANES] == gi, part, col)
        cols.append(col)
    wsum = jnp.concatenate(cols, axis=1)
    pos = step_in_batch * STEP_ROWS + lax.broadcasted_iota(jnp.int32, (STEP_ROWS, 1), 0)
    inv_count = jnp.maximum(1.0 / (pos.astype(_F32) + 1.0), inv_win)
    d = (wsum * inv_count - p).astype(_BF16)
    mix = _dot(d, wgrp_ref[0]) * sc_ref[0]
    att = _cross_attention(q, kbd_ref, vbd_ref)
    cat = jnp.concatenate([mix.astype(_BF16), att.astype(_BF16)], axis=1)
    o_ref[...] = x_ref[...] + _dot(cat, wout_ref[0])


def _mixer_call(kernel_fn, name, x, layer, kbd, vbd, weights_before, w_out, seq, casts):
    t = x.shape[0]
    n_steps = t // STEP_ROWS
    steps_per_batch = seq // STEP_ROWS
    row = pl.BlockSpec((STEP_ROWS, D_MODEL), lambda i: (i, 0))
    kbd_spec = pl.BlockSpec((1, 1, XW, HEADS * N_MEM),
                            lambda i: (layer, i // steps_per_batch, 0, 0))
    vbd_spec = pl.BlockSpec((1, 1, HEADS * N_MEM, 2 * PAIR_W),
                            lambda i: (layer, i // steps_per_batch, 0, 0))
    jobs = [_cast_job(w, li, n_steps) for w, li in casts]
    outs = pl.pallas_call(
        functools.partial(kernel_fn, steps_per_batch=steps_per_batch, n_cast=len(casts)),
        grid=(n_steps,),
        in_specs=[row] + [_layer_resident(w.shape, li) for w, li in weights_before]
        + [kbd_spec, vbd_spec, _layer_resident(w_out.shape, 0)] + [j[0] for j in jobs],
        out_specs=[row] + [j[1] for j in jobs],
        out_shape=[jax.ShapeDtypeStruct((t, D_MODEL), _F32)] + [j[2] for j in jobs],
        scratch_shapes=[pltpu.VMEM((STEP_ROWS, D_MODEL), _BF16),
                        pltpu.VMEM((CARRY_ROWS, MW), _F32)],
        compiler_params=pltpu.CompilerParams(
            dimension_semantics=("arbitrary",), vmem_limit_bytes=VMEM_LIMIT_BYTES),
        name=name,
    )(x, *[w for w, _ in weights_before], kbd, vbd, w_out, *[w for w, _ in casts])
    return outs[0], outs[1:]


def _block_diag(w_group):
    nl, g, d, _ = w_group.shape
    out = jnp.zeros((nl, g * d, g * d), w_group.dtype)
    for gi in range(g):
        out = lax.dynamic_update_slice(out, w_group[:, gi], (0, gi * d, gi * d))
    return out


def kernel(x, mem, ffn1_norm, ffn1_w_gu, ffn1_w_down, mix_norm, mem_norm, w_kv, w_out,
           conv_w_in, conv_w, pool_w_in, pool_w_group, pool_scale,
           ffn2_norm, ffn2_w_gu, ffn2_w_down, final_norm):
    batch, seq, _ = x.shape
    depth = ffn1_norm.shape[0]
    assert seq % STEP_ROWS == 0 and STEP_ROWS >= CARRY_ROWS >= max(POOL_WINDOWS) - 1

    vec = lambda v: v.reshape(v.shape[0], 1, v.shape[-1])
    fin = final_norm.reshape(1, D_MODEL)
    g1, g2, mix_g = vec(ffn1_norm), vec(ffn2_norm), vec(mix_norm)
    pool_grp, pool_sc = _block_diag(pool_w_group).astype(_BF16), vec(pool_scale)

    kbd, vbd, (wgu, wd) = _kv_call(mem, mem_norm, w_kv, [(ffn1_w_gu, 0), (ffn1_w_down, 0)])
    mixer_casts = []
    ia = ib = 0
    for i in range(depth):
        mixer_casts += [(w_out, i), (conv_w_in, ia) if i % 2 == 0 else (pool_w_in, ib)]
        ia, ib = ia + (i % 2 == 0), ib + (i % 2 == 1)

    xt = x.reshape(batch * seq, D_MODEL)
    mixer_w = None
    ia = ib = 0
    for i in range(depth):
        xt, cast = _ffn_call(xt, g1, i, wgu, wd, fin, final_norm=False,
                             casts=mixer_casts if i == 0 else ())
        mixer_w = cast if i == 0 else mixer_w
        w_out_i, w_in_i = mixer_w[2 * i], mixer_w[2 * i + 1]
        casts = [(ffn2_w_gu, i), (ffn2_w_down, i)]
        if i + 1 < depth:
            casts += [(ffn1_w_gu, i + 1), (ffn1_w_down, i + 1)]
        if i % 2 == 0:
            xt, cast = _mixer_call(_conv_mixer_kernel, "conv_mixer", xt, i, kbd, vbd,
                                   [(mix_g, i), (w_in_i, 0), (conv_w, ia)], w_out_i, seq,
                                   casts)
            ia += 1
        else:
            xt, cast = _mixer_call(_pool_mixer_kernel, "pool_mixer", xt, i, kbd, vbd,
                                   [(mix_g, i), (w_in_i, 0), (pool_grp, ib), (pool_sc, ib)],
                                   w_out_i, seq, casts)
            ib += 1
        xt, _ = _ffn_call(xt, g2, i, cast[0], cast[1], fin, final_norm=(i == depth - 1))
        if i + 1 < depth:
            wgu, wd = cast[2], cast[3]
    return xt.reshape(batch, seq, D_MODEL)
```

```python
import functools

import jax
import jax.numpy as jnp
from jax import lax
from jax.experimental import pallas as pl
from jax.experimental.pallas import tpu as pltpu

D_MODEL = 1024
N_MEM = 256
HEADS = 4
HEAD_DIM = 64
XW = HEADS * HEAD_DIM
PAIR_W = 2 * HEAD_DIM
MW = D_MODEL - XW
CONV_WIDTH = 3
POOL_WINDOWS = (2, 4, 8, 16)
POOL_GROUP_DIM = MW // len(POOL_WINDOWS)
D_FF = 2816
RMS_EPS = 1e-6

STEP_ROWS = 1024
FF_CHUNKS = ((0, 512), (512, 512), (1024, 512), (1536, 512), (2048, 512), (2560, 256))
CARRY_ROWS = 16
BF16_SUBLANES = 16
LANES = 128
MXU_TILE = 256
VMEM_LIMIT_BYTES = 56 * 1024 * 1024

_BF16 = jnp.bfloat16
_F32 = jnp.float32


def _rms_norm(x, g):
    y = x * lax.rsqrt(jnp.mean(x * x, axis=-1, keepdims=True) + RMS_EPS)
    return y * g


def _dot(a, b):
    return jnp.dot(a, b, preferred_element_type=_F32)


def _layer_resident(shape, layer):
    nd = len(shape) - 1
    return pl.BlockSpec((1,) + tuple(shape[1:]), lambda *_: (layer,) + (0,) * nd,
                        pipeline_mode=pl.Buffered(1))


def _norm_halves(x_ref, g, h_ref):
    half = STEP_ROWS // 2
    halves = (pl.ds(0, half // 2), pl.ds(half // 2, half // 2), pl.ds(half, half))
    for rows in halves:
        h_ref[rows, :] = _rms_norm(x_ref[rows, :], g).astype(_BF16)
    return halves


def _cast_job(w, layer, n_steps, step_of=lambda i: i):
    _, r, c = w.shape
    n_blocks = n_steps
    while r % n_blocks or (r // n_blocks) % BF16_SUBLANES:
        n_blocks //= 2
    block = (1, r // n_blocks, c)
    src = pl.BlockSpec(block, lambda *i: (layer, step_of(*i) * n_blocks // n_steps, 0))
    dst = pl.BlockSpec(block, lambda *i: (0, step_of(*i) * n_blocks // n_steps, 0))
    return src, dst, jax.ShapeDtypeStruct((1, r, c), _BF16)


def _cast_slices(srcs, dsts):
    for src, dst in zip(srcs, dsts):
        dst[...] = src[...].astype(_BF16)


def _kv_kernel(mem_ref, g_ref, wkv_ref, *refs, n_cast):
    kbd_ref, vbd_ref = refs[n_cast:n_cast + 2]
    mem_h = _rms_norm(mem_ref[0], g_ref[0])
    kv = _dot(mem_h, wkv_ref[0])
    k = kv[:, :XW] * (HEAD_DIM ** -0.5)
    v = kv[:, XW:]
    kt = jnp.concatenate([k.T] * HEADS, axis=1)
    r = lax.broadcasted_iota(jnp.int32, kt.shape, 0) // HEAD_DIM
    c = lax.broadcasted_iota(jnp.int32, kt.shape, 1) // N_MEM
    kbd_ref[0, 0] = jnp.where(r == c, kt, 0.0).astype(_BF16)
    for pr in range(HEADS // 2):
        vp = v[:, pr * PAIR_W:(pr + 1) * PAIR_W]
        vp2 = jnp.concatenate([vp, vp], axis=0)
        r = lax.broadcasted_iota(jnp.int32, vp2.shape, 0) // N_MEM
        c = lax.broadcasted_iota(jnp.int32, vp2.shape, 1) // HEAD_DIM
        own = r == c
        rows = pl.ds(pr * 2 * N_MEM, 2 * N_MEM)
        vbd_ref[0, 0, rows, :PAIR_W] = jnp.where(own, vp2, 0.0).astype(_BF16)
        vbd_ref[0, 0, rows, PAIR_W:] = jnp.where(own, 1.0, 0.0).astype(_BF16)
    _cast_slices(refs[:n_cast], refs[n_cast + 2:])


def _kv_call(mem, mem_norm, w_kv, casts):
    depth, batch = mem_norm.shape[0], mem.shape[0]
    jobs = [_cast_job(w, li, depth * batch, lambda l, b: l * batch + b) for w, li in casts]
    outs = pl.pallas_call(
        functools.partial(_kv_kernel, n_cast=len(casts)),
        grid=(depth, batch),
        in_specs=[
            pl.BlockSpec((1, N_MEM, D_MODEL), lambda l, b: (b, 0, 0)),
            pl.BlockSpec((1, 1, D_MODEL), lambda l, b: (l, 0, 0)),
            pl.BlockSpec((1, D_MODEL, 2 * XW), lambda l, b: (l, 0, 0)),
        ] + [j[0] for j in jobs],
        out_specs=[
            pl.BlockSpec((1, 1, XW, HEADS * N_MEM), lambda l, b: (l, b, 0, 0)),
            pl.BlockSpec((1, 1, HEADS * N_MEM, 2 * PAIR_W), lambda l, b: (l, b, 0, 0)),
        ] + [j[1] for j in jobs],
        out_shape=[
            jax.ShapeDtypeStruct((depth, batch, XW, HEADS * N_MEM), _BF16),
            jax.ShapeDtypeStruct((depth, batch, HEADS * N_MEM, 2 * PAIR_W), _BF16),
        ] + [j[2] for j in jobs],
        compiler_params=pltpu.CompilerParams(
            dimension_semantics=("arbitrary", "arbitrary"), vmem_limit_bytes=VMEM_LIMIT_BYTES),
        name="kv_proj",
    )(mem, mem_norm.reshape(depth, 1, D_MODEL), w_kv, *[w for w, _ in casts])
    return outs[0], outs[1], outs[2:]


def _ffn_kernel(x_ref, g_ref, wgu_ref, wd_ref, fin_ref, *refs, final_norm, n_cast):
    o_ref, (h_ref, act_ref) = refs[n_cast], refs[2 * n_cast + 1:]
    halves = _norm_halves(x_ref, g_ref[0], h_ref)
    for ci, (c0, cw) in enumerate(FF_CHUNKS):
        for rows in (halves if ci == 0 else (pl.ds(0, STEP_ROWS),)):
            gate = _dot(h_ref[rows, :], wgu_ref[0, :, c0:c0 + cw])
            up = _dot(h_ref[rows, :], wgu_ref[0, :, D_FF + c0:D_FF + c0 + cw])
            act_ref[rows, c0:c0 + cw] = (gate * jax.nn.sigmoid(gate) * up).astype(_BF16)
    if final_norm:
        half = STEP_ROWS // 2
        for rows in (pl.ds(0, half), pl.ds(half, half)):
            y = x_ref[rows, :] + 0.5 * _dot(act_ref[rows, :], wd_ref[0])
            o_ref[rows, :] = _rms_norm(y, fin_ref[...])
    else:
        o_ref[...] = x_ref[...] + 0.5 * _dot(act_ref[...], wd_ref[0])
    _cast_slices(refs[:n_cast], refs[n_cast + 1:2 * n_cast + 1])


def _ffn_call(x, g, g_layer, wgu, wd, fin, *, final_norm, casts=()):
    t = x.shape[0]
    n_steps = t // STEP_ROWS
    row = pl.BlockSpec((STEP_ROWS, D_MODEL), lambda i: (i, 0))
    jobs = [_cast_job(w, li, n_steps) for w, li in casts]
    outs = pl.pallas_call(
        functools.partial(_ffn_kernel, final_norm=final_norm, n_cast=len(casts)),
        grid=(n_steps,),
        in_specs=[
            row,
            _layer_resident(g.shape, g_layer),
            _layer_resident(wgu.shape, 0),
            _layer_resident(wd.shape, 0),
            pl.BlockSpec(fin.shape, lambda i: (0, 0), pipeline_mode=pl.Buffered(1)),
        ] + [j[0] for j in jobs],
        out_specs=[row] + [j[1] for j in jobs],
        out_shape=[jax.ShapeDtypeStruct((t, D_MODEL), _F32)] + [j[2] for j in jobs],
        scratch_shapes=[pltpu.VMEM((STEP_ROWS, D_MODEL), _BF16),
                        pltpu.VMEM((STEP_ROWS, D_FF), _BF16)],
        compiler_params=pltpu.CompilerParams(
            dimension_semantics=("arbitrary",), vmem_limit_bytes=VMEM_LIMIT_BYTES),
        name="ffn",
    )(x, g, wgu, wd, fin, *[w for w, _ in casts])
    return outs[0], outs[1:]


def _cross_attention(q, kbd_ref, vbd_ref):
    s = _dot(q.astype(_BF16), kbd_ref[0, 0])
    es = []
    for hd in range(HEADS):
        sh = s[:, hd * N_MEM:(hd + 1) * N_MEM]
        es.append(jnp.exp(sh - jnp.max(sh, axis=-1, keepdims=True)).astype(_BF16))
    att = []
    for pr in range(HEADS // 2):
        pv = _dot(jnp.concatenate(es[2 * pr:2 * pr + 2], axis=1),
                  vbd_ref[0, 0, pl.ds(pr * 2 * N_MEM, 2 * N_MEM), :])
        att.append(pv[:, :PAIR_W] / pv[:, PAIR_W:])
    return jnp.concatenate(att, axis=1)


def _carry_in(carry_ref, step_in_batch):
    @pl.when(pl.program_id(0) == 0)
    def _():
        carry_ref[...] = jnp.zeros_like(carry_ref)

    return jnp.where(step_in_batch == 0, 0.0, carry_ref[...])


def _conv_mixer_kernel(x_ref, g_ref, win_ref, cw_ref, kbd_ref, vbd_ref, wout_ref, *refs,
                       steps_per_batch, n_cast):
    o_ref, (h_ref, carry_ref) = refs[n_cast], refs[2 * n_cast + 1:]
    carry = _carry_in(carry_ref, pl.program_id(0) % steps_per_batch)
    _cast_slices(refs[:n_cast], refs[n_cast + 1:2 * n_cast + 1])
    _norm_halves(x_ref, g_ref[0], h_ref)
    q = _dot(h_ref[...], win_ref[0, :, 3 * MW:])
    gcv = _dot(h_ref[...], win_ref[0, :, MW:3 * MW])
    cv = gcv[:, :MW] * gcv[:, MW:]
    gate_b = _dot(h_ref[...], win_ref[0, :, :MW])
    row = lax.broadcasted_iota(jnp.int32, (STEP_ROWS, 1), 0)
    prev1 = carry[CARRY_ROWS - 1:CARRY_ROWS, :]
    prev2 = carry[CARRY_ROWS - 2:CARRY_ROWS - 1, :]
    cv1 = jnp.where(row == 0, prev1, pltpu.roll(cv, 1, axis=0))
    cv2 = jnp.where(row == 0, prev2, jnp.where(row == 1, prev1, pltpu.roll(cv, 2, axis=0)))
    carry_ref[...] = cv[STEP_ROWS - CARRY_ROWS:, :]
    mix = gate_b * (cw_ref[0, 0:1, :] * cv2 + cw_ref[0, 1:2, :] * cv1 + cw_ref[0, 2:3, :] * cv)
    att = _cross_attention(q, kbd_ref, vbd_ref)
    cat = jnp.concatenate([mix.astype(_BF16), att.astype(_BF16)], axis=1)
    o_ref[...] = x_ref[...] + _dot(cat, wout_ref[0])


def _pool_mixer_kernel(x_ref, g_ref, win_ref, wgrp_ref, sc_ref, kbd_ref, vbd_ref, wout_ref,
                       *refs, steps_per_batch, n_cast):
    o_ref, (h_ref, carry_ref) = refs[n_cast], refs[2 * n_cast + 1:]
    step_in_batch = pl.program_id(0) % steps_per_batch
    carry = _carry_in(carry_ref, step_in_batch)
    _cast_slices(refs[:n_cast], refs[n_cast + 1:2 * n_cast + 1])
    _norm_halves(x_ref, g_ref[0], h_ref)
    z = _dot(h_ref[...], win_ref[0])
    p = z[:, :MW]
    q = z[:, MW:]
    ext = jnp.concatenate([carry, p], axis=0)
    carry_ref[...] = p[STEP_ROWS - CARRY_ROWS:, :]
    lane_group = lax.broadcasted_iota(jnp.int32, (1, MW), 1) // POOL_GROUP_DIM
    inv_win = jnp.zeros((1, MW), _F32)
    levels = []
    acc, base, span = ext, 0, 1
    for gi, win in enumerate(POOL_WINDOWS):
        first = gi * POOL_GROUP_DIM // LANES * LANES
        acc, base = acc[:, first - base:], first
        while span < win:
            acc = acc + pltpu.roll(acc, span, axis=0)
            span *= 2
        levels.append((acc, base))
        inv_win = jnp.where(lane_group == gi, 1.0 / win, inv_win)
    cols = []
    for c0 in range(0, MW, LANES):
        col = None
        for gi in range(c0 // POOL_GROUP_DIM, (c0 + LANES - 1) // POOL_GROUP_DIM + 1):
            sums, first = levels[gi]
            part = sums[CARRY_ROWS:, c0 - first:c0 - first + LANES]
            col = part if col is None else jnp.where(
                lane_group[:, c0:c0 + LANES] == gi, part, col)
        cols.append(col)
    wsum = jnp.concatenate(cols, axis=1)
    pos = step_in_batch * STEP_ROWS + lax.broadcasted_iota(jnp.int32, (STEP_ROWS, 1), 0)
    inv_count = jnp.maximum(1.0 / (pos.astype(_F32) + 1.0), inv_win)
    d = (wsum * inv_count - p).astype(_BF16)
    mix_cols = []
    for c0 in range(0, MW, MXU_TILE):
        first_group, last_group = c0 // POOL_GROUP_DIM, (c0 + MXU_TILE - 1) // POOL_GROUP_DIM
        k0 = first_group * POOL_GROUP_DIM // LANES * LANES
        k1 = min(MW, -(-(last_group + 1) * POOL_GROUP_DIM // LANES) * LANES)
        mix_cols.append(_dot(d[:, k0:k1], wgrp_ref[0, k0:k1, c0:c0 + MXU_TILE]))
    mix = jnp.concatenate(mix_cols, axis=1) * sc_ref[0]
    att = _cross_attention(q, kbd_ref, vbd_ref)
    cat = jnp.concatenate([mix.astype(_BF16), att.astype(_BF16)], axis=1)
    o_ref[...] = x_ref[...] + _dot(cat, wout_ref[0])


def _mixer_call(kernel_fn, name, x, layer, kbd, vbd, weights_before, w_out, seq, casts):
    t = x.shape[0]
    n_steps = t // STEP_ROWS
    steps_per_batch = seq // STEP_ROWS
    row = pl.BlockSpec((STEP_ROWS, D_MODEL), lambda i: (i, 0))
    kbd_spec = pl.BlockSpec((1, 1, XW, HEADS * N_MEM),
                            lambda i: (layer, i // steps_per_batch, 0, 0))
    vbd_spec = pl.BlockSpec((1, 1, HEADS * N_MEM, 2 * PAIR_W),
                            lambda i: (layer, i // steps_per_batch, 0, 0))
    jobs = [_cast_job(w, li, n_steps) for w, li in casts]
    outs = pl.pallas_call(
        functools.partial(kernel_fn, steps_per_batch=steps_per_batch, n_cast=len(casts)),
        grid=(n_steps,),
        in_specs=[row] + [_layer_resident(w.shape, li) for w, li in weights_before]
        + [kbd_spec, vbd_spec, _layer_resident(w_out.shape, 0)] + [j[0] for j in jobs],
        out_specs=[row] + [j[1] for j in jobs],
        out_shape=[jax.ShapeDtypeStruct((t, D_MODEL), _F32)] + [j[2] for j in jobs],
        scratch_shapes=[pltpu.VMEM((STEP_ROWS, D_MODEL), _BF16),
                        pltpu.VMEM((CARRY_ROWS, MW), _F32)],
        compiler_params=pltpu.CompilerParams(
            dimension_semantics=("arbitrary",), vmem_limit_bytes=VMEM_LIMIT_BYTES),
        name=name,
    )(x, *[w for w, _ in weights_before], kbd, vbd, w_out, *[w for w, _ in casts])
    return outs[0], outs[1:]


def _block_diag(w_group):
    nl, g, d, _ = w_group.shape
    out = jnp.zeros((nl, g * d, g * d), w_group.dtype)
    for gi in range(g):
        out = lax.dynamic_update_slice(out, w_group[:, gi], (0, gi * d, gi * d))
    return out


def kernel(x, mem, ffn1_norm, ffn1_w_gu, ffn1_w_down, mix_norm, mem_norm, w_kv, w_out,
           conv_w_in, conv_w, pool_w_in, pool_w_group, pool_scale,
           ffn2_norm, ffn2_w_gu, ffn2_w_down, final_norm):
    batch, seq, _ = x.shape
    depth = ffn1_norm.shape[0]
    assert seq % STEP_ROWS == 0 and STEP_ROWS >= CARRY_ROWS >= max(POOL_WINDOWS) - 1

    vec = lambda v: v.reshape(v.shape[0], 1, v.shape[-1])
    fin = final_norm.reshape(1, D_MODEL)
    g1, g2, mix_g = vec(ffn1_norm), vec(ffn2_norm), vec(mix_norm)
    pool_grp, pool_sc = _block_diag(pool_w_group).astype(_BF16), vec(pool_scale)

    kbd, vbd, (wgu, wd) = _kv_call(mem, mem_norm, w_kv, [(ffn1_w_gu, 0), (ffn1_w_down, 0)])
    mixer_casts = []
    ia = ib = 0
    for i in range(depth):
        mixer_casts += [(w_out, i), (conv_w_in, ia) if i % 2 == 0 else (pool_w_in, ib)]
        ia, ib = ia + (i % 2 == 0), ib + (i % 2 == 1)

    xt = x.reshape(batch * seq, D_MODEL)
    mixer_w = None
    ia = ib = 0
    for i in range(depth):
        xt, cast = _ffn_call(xt, g1, i, wgu, wd, fin, final_norm=False,
                             casts=mixer_casts if i == 0 else ())
        mixer_w = cast if i == 0 else mixer_w
        w_out_i, w_in_i = mixer_w[2 * i], mixer_w[2 * i + 1]
        casts = [(ffn2_w_gu, i), (ffn2_w_down, i)]
        if i + 1 < depth:
            casts += [(ffn1_w_gu, i + 1), (ffn1_w_down, i + 1)]
        if i % 2 == 0:
            xt, cast = _mixer_call(_conv_mixer_kernel, "conv_mixer", xt, i, kbd, vbd,
                                   [(mix_g, i), (w_in_i, 0), (conv_w, ia)], w_out_i, seq,
                                   casts)
            ia += 1
        else:
            xt, cast = _mixer_call(_pool_mixer_kernel, "pool_mixer", xt, i, kbd, vbd,
                                   [(mix_g, i), (w_in_i, 0), (pool_grp, ib), (pool_sc, ib)],
                                   w_out_i, seq, casts)
            ib += 1
        xt, _ = _ffn_call(xt, g2, i, cast[0], cast[1], fin, final_norm=(i == depth - 1))
        if i + 1 < depth:
            wgu, wd = cast[2], cast[3]
    return xt.reshape(batch, seq, D_MODEL)
```

```python
import functools

import jax
import jax.numpy as jnp
from jax import lax
from jax.experimental import pallas as pl
from jax.experimental.pallas import tpu as pltpu

D_MODEL = 1024
N_MEM = 256
HEADS = 4
HEAD_DIM = 64
XW = HEADS * HEAD_DIM
PAIR_W = 2 * HEAD_DIM
MW = D_MODEL - XW
CONV_WIDTH = 3
POOL_WINDOWS = (2, 4, 8, 16)
POOL_GROUP_DIM = MW // len(POOL_WINDOWS)
D_FF = 2816
RMS_EPS = 1e-6

STEP_ROWS = 1024
FF_CHUNKS = ((0, 512), (512, 512), (1024, 512), (1536, 512), (2048, 512), (2560, 256))
CARRY_ROWS = 16
BF16_SUBLANES = 16
LANES = 128
MXU_TILE = 256
VMEM_LIMIT_BYTES = 56 * 1024 * 1024

_BF16 = jnp.bfloat16
_F32 = jnp.float32


def _rms_norm(x, g):
    y = x * lax.rsqrt(jnp.mean(x * x, axis=-1, keepdims=True) + RMS_EPS)
    return y * g


def _dot(a, b):
    return jnp.dot(a, b, preferred_element_type=_F32)


def _layer_resident(shape, layer):
    nd = len(shape) - 1
    return pl.BlockSpec((1,) + tuple(shape[1:]), lambda *_: (layer,) + (0,) * nd,
                        pipeline_mode=pl.Buffered(1))


def _norm_parts(x_ref, g, h_ref):
    quarter = STEP_ROWS // 4
    parts = (pl.ds(0, quarter), pl.ds(quarter, quarter), pl.ds(2 * quarter, 2 * quarter))
    for rows in parts:
        h_ref[rows, :] = _rms_norm(x_ref[rows, :], g).astype(_BF16)
    return parts


def _cast_job(w, layer, n_steps, step_of=lambda i: i):
    _, r, c = w.shape
    n_blocks = n_steps
    while r % n_blocks or (r // n_blocks) % BF16_SUBLANES:
        n_blocks //= 2
    block = (1, r // n_blocks, c)
    src = pl.BlockSpec(block, lambda *i: (layer, step_of(*i) * n_blocks // n_steps, 0))
    dst = pl.BlockSpec(block, lambda *i: (0, step_of(*i) * n_blocks // n_steps, 0))
    return src, dst, jax.ShapeDtypeStruct((1, r, c), _BF16)


def _cast_slices(srcs, dsts):
    for src, dst in zip(srcs, dsts):
        dst[...] = src[...].astype(_BF16)


def _kv_kernel(mem_ref, g_ref, wkv_ref, *refs, n_cast):
    kbd_ref, vbd_ref = refs[n_cast:n_cast + 2]
    mem_h = _rms_norm(mem_ref[0], g_ref[0])
    kv = _dot(mem_h, wkv_ref[0])
    k = kv[:, :XW] * (HEAD_DIM ** -0.5)
    v = kv[:, XW:]
    kt = jnp.concatenate([k.T] * HEADS, axis=1)
    r = lax.broadcasted_iota(jnp.int32, kt.shape, 0) // HEAD_DIM
    c = lax.broadcasted_iota(jnp.int32, kt.shape, 1) // N_MEM
    kbd_ref[0, 0] = jnp.where(r == c, kt, 0.0).astype(_BF16)
    for pr in range(HEADS // 2):
        vp = v[:, pr * PAIR_W:(pr + 1) * PAIR_W]
        vp2 = jnp.concatenate([vp, vp], axis=0)
        r = lax.broadcasted_iota(jnp.int32, vp2.shape, 0) // N_MEM
        c = lax.broadcasted_iota(jnp.int32, vp2.shape, 1) // HEAD_DIM
        own = r == c
        rows = pl.ds(pr * 2 * N_MEM, 2 * N_MEM)
        vbd_ref[0, 0, rows, :PAIR_W] = jnp.where(own, vp2, 0.0).astype(_BF16)
        vbd_ref[0, 0, rows, PAIR_W:] = jnp.where(own, 1.0, 0.0).astype(_BF16)
    _cast_slices(refs[:n_cast], refs[n_cast + 2:])


def _kv_call(mem, mem_norm, w_kv, casts):
    depth, batch = mem_norm.shape[0], mem.shape[0]
    jobs = [_cast_job(w, li, depth * batch, lambda l, b: l * batch + b) for w, li in casts]
    outs = pl.pallas_call(
        functools.partial(_kv_kernel, n_cast=len(casts)),
        grid=(depth, batch),
        in_specs=[
            pl.BlockSpec((1, N_MEM, D_MODEL), lambda l, b: (b, 0, 0)),
            pl.BlockSpec((1, 1, D_MODEL), lambda l, b: (l, 0, 0)),
            pl.BlockSpec((1, D_MODEL, 2 * XW), lambda l, b: (l, 0, 0)),
        ] + [j[0] for j in jobs],
        out_specs=[
            pl.BlockSpec((1, 1, XW, HEADS * N_MEM), lambda l, b: (l, b, 0, 0)),
            pl.BlockSpec((1, 1, HEADS * N_MEM, 2 * PAIR_W), lambda l, b: (l, b, 0, 0)),
        ] + [j[1] for j in jobs],
        out_shape=[
            jax.ShapeDtypeStruct((depth, batch, XW, HEADS * N_MEM), _BF16),
            jax.ShapeDtypeStruct((depth, batch, HEADS * N_MEM, 2 * PAIR_W), _BF16),
        ] + [j[2] for j in jobs],
        compiler_params=pltpu.CompilerParams(
            dimension_semantics=("arbitrary", "arbitrary"), vmem_limit_bytes=VMEM_LIMIT_BYTES),
        name="kv_proj",
    )(mem, mem_norm.reshape(depth, 1, D_MODEL), w_kv, *[w for w, _ in casts])
    return outs[0], outs[1], outs[2:]


def _ffn_kernel(x_ref, g_ref, wgu_ref, wd_ref, fin_ref, *refs, final_norm, n_cast):
    o_ref, (h_ref, act_ref) = refs[n_cast], refs[2 * n_cast + 1:]
    parts = _norm_parts(x_ref, g_ref[0], h_ref)
    for ci, (c0, cw) in enumerate(FF_CHUNKS):
        for rows in (parts if ci == 0 else (pl.ds(0, STEP_ROWS),)):
            gate = _dot(h_ref[rows, :], wgu_ref[0, :, c0:c0 + cw])
            up = _dot(h_ref[rows, :], wgu_ref[0, :, D_FF + c0:D_FF + c0 + cw])
            act_ref[rows, c0:c0 + cw] = (gate * jax.nn.sigmoid(gate) * up).astype(_BF16)
    if final_norm:
        half = STEP_ROWS // 2
        for rows in (pl.ds(0, half), pl.ds(half, half)):
            y = x_ref[rows, :] + 0.5 * _dot(act_ref[rows, :], wd_ref[0])
            o_ref[rows, :] = _rms_norm(y, fin_ref[...])
    else:
        o_ref[...] = x_ref[...] + 0.5 * _dot(act_ref[...], wd_ref[0])
    _cast_slices(refs[:n_cast], refs[n_cast + 1:2 * n_cast + 1])


def _ffn_call(x, g, g_layer, wgu, wd, fin, *, final_norm, casts=()):
    t = x.shape[0]
    n_steps = t // STEP_ROWS
    row = pl.BlockSpec((STEP_ROWS, D_MODEL), lambda i: (i, 0))
    jobs = [_cast_job(w, li, n_steps) for w, li in casts]
    outs = pl.pallas_call(
        functools.partial(_ffn_kernel, final_norm=final_norm, n_cast=len(casts)),
        grid=(n_steps,),
        in_specs=[
            row,
            _layer_resident(g.shape, g_layer),
            _layer_resident(wgu.shape, 0),
            _layer_resident(wd.shape, 0),
            pl.BlockSpec(fin.shape, lambda i: (0, 0), pipeline_mode=pl.Buffered(1)),
        ] + [j[0] for j in jobs],
        out_specs=[row] + [j[1] for j in jobs],
        out_shape=[jax.ShapeDtypeStruct((t, D_MODEL), _F32)] + [j[2] for j in jobs],
        scratch_shapes=[pltpu.VMEM((STEP_ROWS, D_MODEL), _BF16),
                        pltpu.VMEM((STEP_ROWS, D_FF), _BF16)],
        compiler_params=pltpu.CompilerParams(
            dimension_semantics=("arbitrary",), vmem_limit_bytes=VMEM_LIMIT_BYTES),
        name="ffn",
    )(x, g, wgu, wd, fin, *[w for w, _ in casts])
    return outs[0], outs[1:]


def _cross_attention(q, kbd_ref, vbd_ref):
    s = _dot(q.astype(_BF16), kbd_ref[0, 0])
    es = []
    for hd in range(HEADS):
        sh = s[:, hd * N_MEM:(hd + 1) * N_MEM]
        es.append(jnp.exp(sh - jnp.max(sh, axis=-1, keepdims=True)).astype(_BF16))
    att = []
    for pr in range(HEADS // 2):
        pv = _dot(jnp.concatenate(es[2 * pr:2 * pr + 2], axis=1),
                  vbd_ref[0, 0, pl.ds(pr * 2 * N_MEM, 2 * N_MEM), :])
        att.append(pv[:, :PAIR_W] / pv[:, PAIR_W:])
    return jnp.concatenate(att, axis=1)


def _carry_in(carry_ref, step_in_batch):
    @pl.when(pl.program_id(0) == 0)
    def _():
        carry_ref[...] = jnp.zeros_like(carry_ref)

    return jnp.where(step_in_batch == 0, 0.0, carry_ref[...])


def _conv_mixer_kernel(x_ref, g_ref, win_ref, cw_ref, kbd_ref, vbd_ref, wout_ref, *refs,
                       steps_per_batch, n_cast):
    o_ref, (h_ref, carry_ref) = refs[n_cast], refs[2 * n_cast + 1:]
    carry = _carry_in(carry_ref, pl.program_id(0) % steps_per_batch)
    _cast_slices(refs[:n_cast], refs[n_cast + 1:2 * n_cast + 1])
    _norm_parts(x_ref, g_ref[0], h_ref)
    q = _dot(h_ref[...], win_ref[0, :, 3 * MW:])
    gcv = _dot(h_ref[...], win_ref[0, :, MW:3 * MW])
    cv = gcv[:, :MW] * gcv[:, MW:]
    gate_b = _dot(h_ref[...], win_ref[0, :, :MW])
    row = lax.broadcasted_iota(jnp.int32, (STEP_ROWS, 1), 0)
    prev1 = carry[CARRY_ROWS - 1:CARRY_ROWS, :]
    prev2 = carry[CARRY_ROWS - 2:CARRY_ROWS - 1, :]
    cv1 = jnp.where(row == 0, prev1, pltpu.roll(cv, 1, axis=0))
    cv2 = jnp.where(row == 0, prev2, jnp.where(row == 1, prev1, pltpu.roll(cv, 2, axis=0)))
    carry_ref[...] = cv[STEP_ROWS - CARRY_ROWS:, :]
    mix = gate_b * (cw_ref[0, 0:1, :] * cv2 + cw_ref[0, 1:2, :] * cv1 + cw_ref[0, 2:3, :] * cv)
    att = _cross_attention(q, kbd_ref, vbd_ref)
    cat = jnp.concatenate([mix.astype(_BF16), att.astype(_BF16)], axis=1)
    o_ref[...] = x_ref[...] + _dot(cat, wout_ref[0])


def _pool_mixer_kernel(x_ref, g_ref, win_ref, wgrp_ref, sc_ref, kbd_ref, vbd_ref, wout_ref,
                       *refs, steps_per_batch, n_cast):
    o_ref, (h_ref, carry_ref) = refs[n_cast], refs[2 * n_cast + 1:]
    step_in_batch = pl.program_id(0) % steps_per_batch
    carry = _carry_in(carry_ref, step_in_batch)
    _cast_slices(refs[:n_cast], refs[n_cast + 1:2 * n_cast + 1])
    _norm_parts(x_ref, g_ref[0], h_ref)
    z = _dot(h_ref[...], win_ref[0])
    p = z[:, :MW]
    q = z[:, MW:]
    ext = jnp.concatenate([carry, p], axis=0)
    carry_ref[...] = p[STEP_ROWS - CARRY_ROWS:, :]
    lane_group = lax.broadcasted_iota(jnp.int32, (1, MW), 1) // POOL_GROUP_DIM
    inv_win = jnp.zeros((1, MW), _F32)
    levels = []
    acc, base, span = ext, 0, 1
    for gi, win in enumerate(POOL_WINDOWS):
        first = gi * POOL_GROUP_DIM // LANES * LANES
        acc, base = acc[:, first - base:], first
        while span < win:
            acc = acc + pltpu.roll(acc, span, axis=0)
            span *= 2
        levels.append((acc, base))
        inv_win = jnp.where(lane_group == gi, 1.0 / win, inv_win)
    cols = []
    for c0 in range(0, MW, LANES):
        col = None
        for gi in range(c0 // POOL_GROUP_DIM, (c0 + LANES - 1) // POOL_GROUP_DIM + 1):
            sums, first = levels[gi]
            part = sums[CARRY_ROWS:, c0 - first:c0 - first + LANES]
            col = part if col is None else jnp.where(
                lane_group[:, c0:c0 + LANES] == gi, part, col)
        cols.append(col)
    wsum = jnp.concatenate(cols, axis=1)
    pos = step_in_batch * STEP_ROWS + lax.broadcasted_iota(jnp.int32, (STEP_ROWS, 1), 0)
    inv_count = jnp.maximum(1.0 / (pos.astype(_F32) + 1.0), inv_win)
    d = (wsum * inv_count - p).astype(_BF16)
    mix_cols = []
    for c0 in range(0, MW, MXU_TILE):
        first_group, last_group = c0 // POOL_GROUP_DIM, (c0 + MXU_TILE - 1) // POOL_GROUP_DIM
        k0 = first_group * POOL_GROUP_DIM // LANES * LANES
        k1 = min(MW, -(-(last_group + 1) * POOL_GROUP_DIM // LANES) * LANES)
        mix_cols.append(_dot(d[:, k0:k1], wgrp_ref[0, k0:k1, c0:c0 + MXU_TILE]))
    mix = jnp.concatenate(mix_cols, axis=1) * sc_ref[0]
    att = _cross_attention(q, kbd_ref, vbd_ref)
    cat = jnp.concatenate([mix.astype(_BF16), att.astype(_BF16)], axis=1)
    o_ref[...] = x_ref[...] + _dot(cat, wout_ref[0])


def _mixer_call(kernel_fn, name, x, layer, kbd, vbd, weights_before, w_out, seq, casts):
    t = x.shape[0]
    n_steps = t // STEP_ROWS
    steps_per_batch = seq // STEP_ROWS
    row = pl.BlockSpec((STEP_ROWS, D_MODEL), lambda i: (i, 0))
    kbd_spec = pl.BlockSpec((1, 1, XW, HEADS * N_MEM),
                            lambda i: (layer, i // steps_per_batch, 0, 0))
    vbd_spec = pl.BlockSpec((1, 1, HEADS * N_MEM, 2 * PAIR_W),
                            lambda i: (layer, i // steps_per_batch, 0, 0))
    jobs = [_cast_job(w, li, n_steps) for w, li in casts]
    outs = pl.pallas_call(
        functools.partial(kernel_fn, steps_per_batch=steps_per_batch, n_cast=len(casts)),
        grid=(n_steps,),
        in_specs=[row] + [_layer_resident(w.shape, li) for w, li in weights_before]
        + [kbd_spec, vbd_spec, _layer_resident(w_out.shape, 0)] + [j[0] for j in jobs],
        out_specs=[row] + [j[1] for j in jobs],
        out_shape=[jax.ShapeDtypeStruct((t, D_MODEL), _F32)] + [j[2] for j in jobs],
        scratch_shapes=[pltpu.VMEM((STEP_ROWS, D_MODEL), _BF16),
                        pltpu.VMEM((CARRY_ROWS, MW), _F32)],
        compiler_params=pltpu.CompilerParams(
            dimension_semantics=("arbitrary",), vmem_limit_bytes=VMEM_LIMIT_BYTES),
        name=name,
    )(x, *[w for w, _ in weights_before], kbd, vbd, w_out, *[w for w, _ in casts])
    return outs[0], outs[1:]


def _block_diag(w_group):
    _, g, d, _ = w_group.shape
    rows = [jnp.pad(w_group[:, gi], ((0, 0), (0, 0), (gi * d, (g - 1 - gi) * d)))
            for gi in range(g)]
    return jnp.concatenate(rows, axis=1)


def kernel(x, mem, ffn1_norm, ffn1_w_gu, ffn1_w_down, mix_norm, mem_norm, w_kv, w_out,
           conv_w_in, conv_w, pool_w_in, pool_w_group, pool_scale,
           ffn2_norm, ffn2_w_gu, ffn2_w_down, final_norm):
    batch, seq, _ = x.shape
    depth = ffn1_norm.shape[0]
    assert seq % STEP_ROWS == 0 and STEP_ROWS >= CARRY_ROWS >= max(POOL_WINDOWS) - 1
    assert conv_w.shape[1] == CONV_WIDTH == 3

    vec = lambda v: v.reshape(v.shape[0], 1, v.shape[-1])
    fin = final_norm.reshape(1, D_MODEL)
    g1, g2, mix_g = vec(ffn1_norm), vec(ffn2_norm), vec(mix_norm)
    pool_grp, pool_sc = _block_diag(pool_w_group).astype(_BF16), vec(pool_scale)

    kbd, vbd, (wgu, wd) = _kv_call(mem, mem_norm, w_kv, [(ffn1_w_gu, 0), (ffn1_w_down, 0)])
    mixer_casts = []
    ia = ib = 0
    for i in range(depth):
        mixer_casts += [(w_out, i), (conv_w_in, ia) if i % 2 == 0 else (pool_w_in, ib)]
        ia, ib = ia + (i % 2 == 0), ib + (i % 2 == 1)

    xt = x.reshape(batch * seq, D_MODEL)
    mixer_w = None
    ia = ib = 0
    for i in range(depth):
        xt, cast = _ffn_call(xt, g1, i, wgu, wd, fin, final_norm=False,
                             casts=mixer_casts if i == 0 else ())
        mixer_w = cast if i == 0 else mixer_w
        w_out_i, w_in_i = mixer_w[2 * i], mixer_w[2 * i + 1]
        casts = [(ffn2_w_gu, i), (ffn2_w_down, i)]
        if i + 1 < depth:
            casts += [(ffn1_w_gu, i + 1), (ffn1_w_down, i + 1)]
        if i % 2 == 0:
            xt, cast = _mixer_call(_conv_mixer_kernel, "conv_mixer", xt, i, kbd, vbd,
                                   [(mix_g, i), (w_in_i, 0), (conv_w, ia)], w_out_i, seq,
                                   casts)
            ia += 1
        else:
            xt, cast = _mixer_call(_pool_mixer_kernel, "pool_mixer", xt, i, kbd, vbd,
                                   [(mix_g, i), (w_in_i, 0), (pool_grp, ib), (pool_sc, ib)],
                                   w_out_i, seq, casts)
            ib += 1
        xt, _ = _ffn_call(xt, g2, i, cast[0], cast[1], fin, final_norm=(i == depth - 1))
        if i + 1 < depth:
            wgu, wd = cast[2], cast[3]
    return xt.reshape(batch, seq, D_MODEL)
```

```python
import functools

import jax
import jax.numpy as jnp
from jax import lax
from jax.experimental import pallas as pl
from jax.experimental.pallas import tpu as pltpu

D_MODEL = 1024
N_MEM = 256
HEADS = 4
HEAD_DIM = 64
XW = HEADS * HEAD_DIM
PAIR_W = 2 * HEAD_DIM
MW = D_MODEL - XW
CONV_WIDTH = 3
POOL_WINDOWS = (2, 4, 8, 16)
POOL_GROUP_DIM = MW // len(POOL_WINDOWS)
D_FF = 2816
RMS_EPS = 1e-6

STEP_ROWS = 1024
FF_CHUNKS = ((0, 512), (512, 512), (1024, 512), (1536, 512), (2048, 512), (2560, 256))
CARRY_ROWS = 16
BF16_SUBLANES = 16
LANES = 128
MXU_TILE = 256
VMEM_LIMIT_BYTES = 56 * 1024 * 1024

_BF16 = jnp.bfloat16
_F32 = jnp.float32


def _rms_norm(x, g):
    y = x * lax.rsqrt(jnp.mean(x * x, axis=-1, keepdims=True) + RMS_EPS)
    return y * g


def _dot(a, b):
    return jnp.dot(a, b, preferred_element_type=_F32)


def _layer_resident(shape, layer):
    nd = len(shape) - 1
    return pl.BlockSpec((1,) + tuple(shape[1:]), lambda *_: (layer,) + (0,) * nd,
                        pipeline_mode=pl.Buffered(1))


def _norm_parts(x_ref, g, h_ref):
    quarter = STEP_ROWS // 4
    parts = (pl.ds(0, quarter), pl.ds(quarter, quarter), pl.ds(2 * quarter, 2 * quarter))
    for rows in parts:
        h_ref[rows, :] = _rms_norm(x_ref[rows, :], g).astype(_BF16)
    return parts


def _cast_job(cast, n_steps, step_of=lambda i: i):
    w, layer, (col, width) = cast if len(cast) == 3 else (*cast, (0, cast[0].shape[2]))
    r = w.shape[1]
    n_blocks = n_steps
    while r % n_blocks or (r // n_blocks) % BF16_SUBLANES:
        n_blocks //= 2
    block = (1, r // n_blocks, width)
    src = pl.BlockSpec(block, lambda *i: (layer, step_of(*i) * n_blocks // n_steps, col))
    dst = pl.BlockSpec(block, lambda *i: (0, step_of(*i) * n_blocks // n_steps, 0))
    return src, dst, jax.ShapeDtypeStruct((1, r, width), _BF16)


def _ffn_weight_casts(w_gu, w_down, layer):
    return [(w_gu, layer, (0, D_FF)), (w_gu, layer, (1, D_FF)), (w_down, layer)]


def _cast_slices(srcs, dsts):
    for src, dst in zip(srcs, dsts):
        dst[...] = src[...].astype(_BF16)


def _kv_kernel(mem_ref, g_ref, wkv_ref, *refs, n_cast):
    kbd_ref, vbd_ref = refs[n_cast:n_cast + 2]
    mem_h = _rms_norm(mem_ref[0], g_ref[0])
    kv = _dot(mem_h, wkv_ref[0])
    k = kv[:, :XW] * (HEAD_DIM ** -0.5)
    v = kv[:, XW:]
    kt = jnp.concatenate([k.T] * HEADS, axis=1)
    r = lax.broadcasted_iota(jnp.int32, kt.shape, 0) // HEAD_DIM
    c = lax.broadcasted_iota(jnp.int32, kt.shape, 1) // N_MEM
    kbd_ref[0, 0] = jnp.where(r == c, kt, 0.0).astype(_BF16)
    for pr in range(HEADS // 2):
        vp = v[:, pr * PAIR_W:(pr + 1) * PAIR_W]
        vp2 = jnp.concatenate([vp, vp], axis=0)
        r = lax.broadcasted_iota(jnp.int32, vp2.shape, 0) // N_MEM
        c = lax.broadcasted_iota(jnp.int32, vp2.shape, 1) // HEAD_DIM
        own = r == c
        rows = pl.ds(pr * 2 * N_MEM, 2 * N_MEM)
        vbd_ref[0, 0, rows, :PAIR_W] = jnp.where(own, vp2, 0.0).astype(_BF16)
        vbd_ref[0, 0, rows, PAIR_W:] = jnp.where(own, 1.0, 0.0).astype(_BF16)
    _cast_slices(refs[:n_cast], refs[n_cast + 2:])


def _kv_call(mem, mem_norm, w_kv, casts):
    depth, batch = mem_norm.shape[0], mem.shape[0]
    jobs = [_cast_job(c, depth * batch, lambda l, b: l * batch + b) for c in casts]
    outs = pl.pallas_call(
        functools.partial(_kv_kernel, n_cast=len(casts)),
        grid=(depth, batch),
        in_specs=[
            pl.BlockSpec((1, N_MEM, D_MODEL), lambda l, b: (b, 0, 0)),
            pl.BlockSpec((1, 1, D_MODEL), lambda l, b: (l, 0, 0)),
            pl.BlockSpec((1, D_MODEL, 2 * XW), lambda l, b: (l, 0, 0)),
        ] + [j[0] for j in jobs],
        out_specs=[
            pl.BlockSpec((1, 1, XW, HEADS * N_MEM), lambda l, b: (l, b, 0, 0)),
            pl.BlockSpec((1, 1, HEADS * N_MEM, 2 * PAIR_W), lambda l, b: (l, b, 0, 0)),
        ] + [j[1] for j in jobs],
        out_shape=[
            jax.ShapeDtypeStruct((depth, batch, XW, HEADS * N_MEM), _BF16),
            jax.ShapeDtypeStruct((depth, batch, HEADS * N_MEM, 2 * PAIR_W), _BF16),
        ] + [j[2] for j in jobs],
        compiler_params=pltpu.CompilerParams(
            dimension_semantics=("arbitrary", "arbitrary"), vmem_limit_bytes=VMEM_LIMIT_BYTES),
        name="kv_proj",
    )(mem, mem_norm.reshape(depth, 1, D_MODEL), w_kv, *[c[0] for c in casts])
    return outs[0], outs[1], outs[2:]


def _ffn_kernel(x_ref, g_ref, wg_ref, wu_ref, wd_ref, fin_ref, *refs, final_norm, n_cast):
    o_ref, (h_ref, act_ref) = refs[n_cast], refs[2 * n_cast + 1:]
    parts = _norm_parts(x_ref, g_ref[0], h_ref)
    for ci, (c0, cw) in enumerate(FF_CHUNKS):
        for rows in (parts if ci == 0 else (pl.ds(0, STEP_ROWS),)):
            gate = _dot(h_ref[rows, :], wg_ref[0, :, c0:c0 + cw])
            up = _dot(h_ref[rows, :], wu_ref[0, :, c0:c0 + cw])
            act_ref[rows, c0:c0 + cw] = (gate * jax.nn.sigmoid(gate) * up).astype(_BF16)
    if final_norm:
        half = STEP_ROWS // 2
        for rows in (pl.ds(0, half), pl.ds(half, half)):
            y = x_ref[rows, :] + 0.5 * _dot(act_ref[rows, :], wd_ref[0])
            o_ref[rows, :] = _rms_norm(y, fin_ref[...])
    else:
        o_ref[...] = x_ref[...] + 0.5 * _dot(act_ref[...], wd_ref[0])
    _cast_slices(refs[:n_cast], refs[n_cast + 1:2 * n_cast + 1])


def _ffn_call(x, g, g_layer, wg, wu, wd, fin, *, final_norm, casts=()):
    t = x.shape[0]
    n_steps = t // STEP_ROWS
    row = pl.BlockSpec((STEP_ROWS, D_MODEL), lambda i: (i, 0))
    jobs = [_cast_job(c, n_steps) for c in casts]
    outs = pl.pallas_call(
        functools.partial(_ffn_kernel, final_norm=final_norm, n_cast=len(casts)),
        grid=(n_steps,),
        in_specs=[
            row,
            _layer_resident(g.shape, g_layer),
            _layer_resident(wg.shape, 0),
            _layer_resident(wu.shape, 0),
            _layer_resident(wd.shape, 0),
            pl.BlockSpec(fin.shape, lambda i: (0, 0), pipeline_mode=pl.Buffered(1)),
        ] + [j[0] for j in jobs],
        out_specs=[row] + [j[1] for j in jobs],
        out_shape=[jax.ShapeDtypeStruct((t, D_MODEL), _F32)] + [j[2] for j in jobs],
        scratch_shapes=[pltpu.VMEM((STEP_ROWS, D_MODEL), _BF16),
                        pltpu.VMEM((STEP_ROWS, D_FF), _BF16)],
        compiler_params=pltpu.CompilerParams(
            dimension_semantics=("arbitrary",), vmem_limit_bytes=VMEM_LIMIT_BYTES),
        name="ffn",
    )(x, g, wg, wu, wd, fin, *[c[0] for c in casts])
    return outs[0], outs[1:]


def _cross_attention(q, kbd_ref, vbd_ref):
    s = _dot(q.astype(_BF16), kbd_ref[0, 0])
    es = []
    for hd in range(HEADS):
        sh = s[:, hd * N_MEM:(hd + 1) * N_MEM]
        es.append(jnp.exp(sh - jnp.max(sh, axis=-1, keepdims=True)).astype(_BF16))
    att = []
    for pr in range(HEADS // 2):
        pv = _dot(jnp.concatenate(es[2 * pr:2 * pr + 2], axis=1),
                  vbd_ref[0, 0, pl.ds(pr * 2 * N_MEM, 2 * N_MEM), :])
        att.append(pv[:, :PAIR_W] / pv[:, PAIR_W:])
    return jnp.concatenate(att, axis=1)


def _carry_in(carry_ref, step_in_batch):
    @pl.when(pl.program_id(0) == 0)
    def _():
        carry_ref[...] = jnp.zeros_like(carry_ref)

    return jnp.where(step_in_batch == 0, 0.0, carry_ref[...])


def _conv_mixer_kernel(x_ref, g_ref, win_ref, cw_ref, kbd_ref, vbd_ref, wout_ref, *refs,
                       steps_per_batch, n_cast):
    o_ref, (h_ref, carry_ref) = refs[n_cast], refs[2 * n_cast + 1:]
    carry = _carry_in(carry_ref, pl.program_id(0) % steps_per_batch)
    _cast_slices(refs[:n_cast], refs[n_cast + 1:2 * n_cast + 1])
    _norm_parts(x_ref, g_ref[0], h_ref)
    q = _dot(h_ref[...], win_ref[0, :, 3 * MW:])
    gcv = _dot(h_ref[...], win_ref[0, :, MW:3 * MW])
    cv = gcv[:, :MW] * gcv[:, MW:]
    gate_b = _dot(h_ref[...], win_ref[0, :, :MW])
    row = lax.broadcasted_iota(jnp.int32, (STEP_ROWS, 1), 0)
    prev1 = carry[CARRY_ROWS - 1:CARRY_ROWS, :]
    prev2 = carry[CARRY_ROWS - 2:CARRY_ROWS - 1, :]
    cv1 = jnp.where(row == 0, prev1, pltpu.roll(cv, 1, axis=0))
    cv2 = jnp.where(row == 0, prev2, jnp.where(row == 1, prev1, pltpu.roll(cv, 2, axis=0)))
    carry_ref[...] = cv[STEP_ROWS - CARRY_ROWS:, :]
    mix = gate_b * (cw_ref[0, 0:1, :] * cv2 + cw_ref[0, 1:2, :] * cv1 + cw_ref[0, 2:3, :] * cv)
    att = _cross_attention(q, kbd_ref, vbd_ref)
    cat = jnp.concatenate([mix.astype(_BF16), att.astype(_BF16)], axis=1)
    o_ref[...] = x_ref[...] + _dot(cat, wout_ref[0])


def _pool_mixer_kernel(x_ref, g_ref, win_ref, wgrp_ref, sc_ref, kbd_ref, vbd_ref, wout_ref,
                       *refs, steps_per_batch, n_cast):
    o_ref, (h_ref, carry_ref) = refs[n_cast], refs[2 * n_cast + 1:]
    step_in_batch = pl.program_id(0) % steps_per_batch
    carry = _carry_in(carry_ref, step_in_batch)
    _cast_slices(refs[:n_cast], refs[n_cast + 1:2 * n_cast + 1])
    _norm_parts(x_ref, g_ref[0], h_ref)
    z = _dot(h_ref[...], win_ref[0])
    p = z[:, :MW]
    q = z[:, MW:]
    ext = jnp.concatenate([carry, p], axis=0)
    carry_ref[...] = p[STEP_ROWS - CARRY_ROWS:, :]
    lane_group = lax.broadcasted_iota(jnp.int32, (1, MW), 1) // POOL_GROUP_DIM
    inv_win = jnp.zeros((1, MW), _F32)
    levels = []
    acc, base, span = ext, 0, 1
    for gi, win in enumerate(POOL_WINDOWS):
        first = gi * POOL_GROUP_DIM // LANES * LANES
        acc, base = acc[:, first - base:], first
        while span < win:
            acc = acc + pltpu.roll(acc, span, axis=0)
            span *= 2
        levels.append((acc, base))
        inv_win = jnp.where(lane_group == gi, 1.0 / win, inv_win)
    cols = []
    for c0 in range(0, MW, LANES):
        col = None
        for gi in range(c0 // POOL_GROUP_DIM, (c0 + LANES - 1) // POOL_GROUP_DIM + 1):
            sums, first = levels[gi]
            part = sums[CARRY_ROWS:, c0 - first:c0 - first + LANES]
            col = part if col is None else jnp.where(
                lane_group[:, c0:c0 + LANES] == gi, part, col)
        cols.append(col)
    wsum = jnp.concatenate(cols, axis=1)
    pos = step_in_batch * STEP_ROWS + lax.broadcasted_iota(jnp.int32, (STEP_ROWS, 1), 0)
    inv_count = jnp.maximum(1.0 / (pos.astype(_F32) + 1.0), inv_win)
    d = (wsum * inv_count - p).astype(_BF16)
    mix_cols = []
    for c0 in range(0, MW, MXU_TILE):
        first_group, last_group = c0 // POOL_GROUP_DIM, (c0 + MXU_TILE - 1) // POOL_GROUP_DIM
        k0 = first_group * POOL_GROUP_DIM // LANES * LANES
        k1 = min(MW, -(-(last_group + 1) * POOL_GROUP_DIM // LANES) * LANES)
        mix_cols.append(_dot(d[:, k0:k1], wgrp_ref[0, k0:k1, c0:c0 + MXU_TILE]))
    mix = jnp.concatenate(mix_cols, axis=1) * sc_ref[0]
    att = _cross_attention(q, kbd_ref, vbd_ref)
    cat = jnp.concatenate([mix.astype(_BF16), att.astype(_BF16)], axis=1)
    o_ref[...] = x_ref[...] + _dot(cat, wout_ref[0])


def _mixer_call(kernel_fn, name, x, layer, kbd, vbd, weights_before, w_out, seq, casts):
    t = x.shape[0]
    n_steps = t // STEP_ROWS
    steps_per_batch = seq // STEP_ROWS
    row = pl.BlockSpec((STEP_ROWS, D_MODEL), lambda i: (i, 0))
    kbd_spec = pl.BlockSpec((1, 1, XW, HEADS * N_MEM),
                            lambda i: (layer, i // steps_per_batch, 0, 0))
    vbd_spec = pl.BlockSpec((1, 1, HEADS * N_MEM, 2 * PAIR_W),
                            lambda i: (layer, i // steps_per_batch, 0, 0))
    jobs = [_cast_job(c, n_steps) for c in casts]
    outs = pl.pallas_call(
        functools.partial(kernel_fn, steps_per_batch=steps_per_batch, n_cast=len(casts)),
        grid=(n_steps,),
        in_specs=[row] + [_layer_resident(w.shape, li) for w, li in weights_before]
        + [kbd_spec, vbd_spec, _layer_resident(w_out.shape, 0)] + [j[0] for j in jobs],
        out_specs=[row] + [j[1] for j in jobs],
        out_shape=[jax.ShapeDtypeStruct((t, D_MODEL), _F32)] + [j[2] for j in jobs],
        scratch_shapes=[pltpu.VMEM((STEP_ROWS, D_MODEL), _BF16),
                        pltpu.VMEM((CARRY_ROWS, MW), _F32)],
        compiler_params=pltpu.CompilerParams(
            dimension_semantics=("arbitrary",), vmem_limit_bytes=VMEM_LIMIT_BYTES),
        name=name,
    )(x, *[w for w, _ in weights_before], kbd, vbd, w_out, *[c[0] for c in casts])
    return outs[0], outs[1:]


def _block_diag(w_group):
    _, g, d, _ = w_group.shape
    rows = [jnp.pad(w_group[:, gi], ((0, 0), (0, 0), (gi * d, (g - 1 - gi) * d)))
            for gi in range(g)]
    return jnp.concatenate(rows, axis=1)


def kernel(x, mem, ffn1_norm, ffn1_w_gu, ffn1_w_down, mix_norm, mem_norm, w_kv, w_out,
           conv_w_in, conv_w, pool_w_in, pool_w_group, pool_scale,
           ffn2_norm, ffn2_w_gu, ffn2_w_down, final_norm):
    batch, seq, _ = x.shape
    depth = ffn1_norm.shape[0]
    assert seq % STEP_ROWS == 0 and STEP_ROWS >= CARRY_ROWS >= max(POOL_WINDOWS) - 1
    assert conv_w.shape[1] == CONV_WIDTH == 3

    vec = lambda v: v.reshape(v.shape[0], 1, v.shape[-1])
    fin = final_norm.reshape(1, D_MODEL)
    g1, g2, mix_g = vec(ffn1_norm), vec(ffn2_norm), vec(mix_norm)
    pool_grp, pool_sc = _block_diag(pool_w_group).astype(_BF16), vec(pool_scale)

    kbd, vbd, ffn_w = _kv_call(mem, mem_norm, w_kv,
                               _ffn_weight_casts(ffn1_w_gu, ffn1_w_down, 0))
    mixer_casts = []
    ia = ib = 0
    for i in range(depth):
        mixer_casts += [(w_out, i), (conv_w_in, ia) if i % 2 == 0 else (pool_w_in, ib)]
        ia, ib = ia + (i % 2 == 0), ib + (i % 2 == 1)

    xt = x.reshape(batch * seq, D_MODEL)
    mixer_w = None
    ia = ib = 0
    for i in range(depth):
        xt, cast = _ffn_call(xt, g1, i, *ffn_w, fin, final_norm=False,
                             casts=mixer_casts if i == 0 else ())
        mixer_w = cast if i == 0 else mixer_w
        w_out_i, w_in_i = mixer_w[2 * i], mixer_w[2 * i + 1]
        casts = _ffn_weight_casts(ffn2_w_gu, ffn2_w_down, i)
        if i + 1 < depth:
            casts += _ffn_weight_casts(ffn1_w_gu, ffn1_w_down, i + 1)
        if i % 2 == 0:
            xt, cast = _mixer_call(_conv_mixer_kernel, "conv_mixer", xt, i, kbd, vbd,
                                   [(mix_g, i), (w_in_i, 0), (conv_w, ia)], w_out_i, seq,
                                   casts)
            ia += 1
        else:
            xt, cast = _mixer_call(_pool_mixer_kernel, "pool_mixer", xt, i, kbd, vbd,
                                   [(mix_g, i), (w_in_i, 0), (pool_grp, ib), (pool_sc, ib)],
                                   w_out_i, seq, casts)
            ib += 1
        xt, _ = _ffn_call(xt, g2, i, *cast[:3], fin, final_norm=(i == depth - 1))
        if i + 1 < depth:
            ffn_w = cast[3:6]
    return xt.reshape(batch, seq, D_MODEL)
```

```python
import functools

import jax
import jax.numpy as jnp
from jax import lax
from jax.experimental import pallas as pl
from jax.experimental.pallas import tpu as pltpu

D_MODEL = 1024
N_MEM = 256
HEADS = 4
HEAD_DIM = 64
XW = HEADS * HEAD_DIM
PAIR_W = 2 * HEAD_DIM
MW = D_MODEL - XW
CONV_WIDTH = 3
POOL_WINDOWS = (2, 4, 8, 16)
POOL_GROUP_DIM = MW // len(POOL_WINDOWS)
D_FF = 2816
RMS_EPS = 1e-6

STEP_ROWS = 1024
FF_CHUNKS = ((0, 512), (512, 512), (1024, 512), (1536, 512), (2048, 512), (2560, 256))
CARRY_ROWS = 16
BF16_SUBLANES = 16
LANES = 128
MXU_TILE = 256
KV_CAST_DEPTH = 3
VMEM_LIMIT_BYTES = 56 * 1024 * 1024

_BF16 = jnp.bfloat16
_F32 = jnp.float32


def _rms_norm(x, g):
    y = x * lax.rsqrt(jnp.mean(x * x, axis=-1, keepdims=True) + RMS_EPS)
    return y * g


def _dot(a, b):
    return jnp.dot(a, b, preferred_element_type=_F32)


def _layer_resident(shape, layer):
    nd = len(shape) - 1
    return pl.BlockSpec((1,) + tuple(shape[1:]), lambda *_: (layer,) + (0,) * nd,
                        pipeline_mode=pl.Buffered(1))


def _norm_parts(x_ref, g, h_ref):
    quarter = STEP_ROWS // 4
    parts = (pl.ds(0, quarter), pl.ds(quarter, quarter), pl.ds(2 * quarter, 2 * quarter))
    for rows in parts:
        h_ref[rows, :] = _rms_norm(x_ref[rows, :], g).astype(_BF16)
    return parts


def _cast_job(cast, n_steps, step_of=lambda i: i):
    w, layer, (col, width) = cast if len(cast) == 3 else (*cast, (0, cast[0].shape[2]))
    r = w.shape[1]
    n_blocks = n_steps
    while r % n_blocks or (r // n_blocks) % BF16_SUBLANES:
        n_blocks //= 2
    block = (1, r // n_blocks, width)
    src = pl.BlockSpec(block, lambda *i: (layer, step_of(*i) * n_blocks // n_steps, col))
    dst = pl.BlockSpec(block, lambda *i: (0, step_of(*i) * n_blocks // n_steps, 0))
    return src, dst, jax.ShapeDtypeStruct((1, r, width), _BF16)


def _ffn_weight_casts(w_gu, w_down, layer):
    return [(w_gu, layer, (0, D_FF)), (w_gu, layer, (1, D_FF)), (w_down, layer)]


def _cast_slices(srcs, dsts):
    for src, dst in zip(srcs, dsts):
        dst[...] = src[...].astype(_BF16)


def _kv_kernel(mem_ref, g_ref, wkv_ref, *refs, casts, n_steps):
    n = len(casts)
    srcs, (kbd_ref, vbd_ref), dsts = refs[:n], refs[n:n + 2], refs[n + 2:2 * n + 2]
    in_bufs, out_bufs = refs[2 * n + 2:3 * n + 2], refs[3 * n + 2:4 * n + 2]
    in_sem, out_sem = refs[4 * n + 2:]
    step = pl.program_id(0) * pl.num_programs(1) + pl.program_id(1)
    ahead = KV_CAST_DEPTH - 1

    def in_copy(j, chunk):
        layer, (col, width) = casts[j]
        rows = in_bufs[j].shape[1]
        slot = lax.rem(chunk, KV_CAST_DEPTH)
        return pltpu.make_async_copy(
            srcs[j].at[layer, pl.ds(chunk * rows, rows), pl.ds(col * width, width)],
            in_bufs[j].at[slot], in_sem.at[j * KV_CAST_DEPTH + slot])

    def out_copy(j, chunk):
        rows = out_bufs[j].shape[1]
        slot = lax.rem(chunk, 2)
        return pltpu.make_async_copy(
            out_bufs[j].at[slot], dsts[j].at[0, pl.ds(chunk * rows, rows), :],
            out_sem.at[j * 2 + slot])

    @pl.when(step == 0)
    def _():
        for j in range(n):
            for chunk in range(ahead):
                in_copy(j, chunk).start()

    @pl.when(step + ahead < n_steps)
    def _():
        for j in range(n):
            in_copy(j, step + ahead).start()

    mem_h = _rms_norm(mem_ref[0], g_ref[0])
    kv = _dot(mem_h, wkv_ref[0])
    k = kv[:, :XW] * (HEAD_DIM ** -0.5)
    v = kv[:, XW:]
    kt = jnp.concatenate([k.T] * HEADS, axis=1)
    r = lax.broadcasted_iota(jnp.int32, kt.shape, 0) // HEAD_DIM
    c = lax.broadcasted_iota(jnp.int32, kt.shape, 1) // N_MEM
    kbd_ref[0, 0] = jnp.where(r == c, kt, 0.0).astype(_BF16)
    for pr in range(HEADS // 2):
        vp = v[:, pr * PAIR_W:(pr + 1) * PAIR_W]
        vp2 = jnp.concatenate([vp, vp], axis=0)
        r = lax.broadcasted_iota(jnp.int32, vp2.shape, 0) // N_MEM
        c = lax.broadcasted_iota(jnp.int32, vp2.shape, 1) // HEAD_DIM
        own = r == c
        rows = pl.ds(pr * 2 * N_MEM, 2 * N_MEM)
        vbd_ref[0, 0, rows, :PAIR_W] = jnp.where(own, vp2, 0.0).astype(_BF16)
        vbd_ref[0, 0, rows, PAIR_W:] = jnp.where(own, 1.0, 0.0).astype(_BF16)
    for j in range(n):
        in_copy(j, step).wait()

    @pl.when(step >= 2)
    def _():
        for j in range(n):
            out_copy(j, step - 2).wait()

    for j in range(n):
        out_bufs[j][lax.rem(step, 2)] = in_bufs[j][lax.rem(step, KV_CAST_DEPTH)].astype(_BF16)
        out_copy(j, step).start()

    @pl.when(step == n_steps - 1)
    def _():
        for j in range(n):
            out_copy(j, step - 1).wait()
            out_copy(j, step).wait()


def _kv_call(mem, mem_norm, w_kv, casts):
    depth, batch = mem_norm.shape[0], mem.shape[0]
    n_steps = depth * batch
    assert n_steps > KV_CAST_DEPTH
    metas, in_bufs, out_bufs, out_shapes = [], [], [], []
    for cast in casts:
        w, layer, (col, width) = cast if len(cast) == 3 else (*cast, (0, cast[0].shape[2]))
        rows = w.shape[1] // n_steps
        assert w.shape[1] % n_steps == 0 and rows % BF16_SUBLANES == 0
        metas.append((layer, (col, width)))
        in_bufs.append(pltpu.VMEM((KV_CAST_DEPTH, rows, width), _F32))
        out_bufs.append(pltpu.VMEM((2, rows, width), _BF16))
        out_shapes.append(jax.ShapeDtypeStruct((1, w.shape[1], width), _BF16))
    any_space = pl.BlockSpec(memory_space=pl.ANY)
    outs = pl.pallas_call(
        functools.partial(_kv_kernel, casts=tuple(metas), n_steps=n_steps),
        grid=(depth, batch),
        in_specs=[
            pl.BlockSpec((1, N_MEM, D_MODEL), lambda l, b: (b, 0, 0)),
            pl.BlockSpec((1, 1, D_MODEL), lambda l, b: (l, 0, 0)),
            pl.BlockSpec((1, D_MODEL, 2 * XW), lambda l, b: (l, 0, 0)),
        ] + [any_space] * len(casts),
        out_specs=[
            pl.BlockSpec((1, 1, XW, HEADS * N_MEM), lambda l, b: (l, b, 0, 0)),
            pl.BlockSpec((1, 1, HEADS * N_MEM, 2 * PAIR_W), lambda l, b: (l, b, 0, 0)),
        ] + [any_space] * len(casts),
        out_shape=[
            jax.ShapeDtypeStruct((depth, batch, XW, HEADS * N_MEM), _BF16),
            jax.ShapeDtypeStruct((depth, batch, HEADS * N_MEM, 2 * PAIR_W), _BF16),
        ] + out_shapes,
        scratch_shapes=in_bufs + out_bufs + [
            pltpu.SemaphoreType.DMA((len(casts) * KV_CAST_DEPTH,)),
            pltpu.SemaphoreType.DMA((len(casts) * 2,))],
        compiler_params=pltpu.CompilerParams(
            dimension_semantics=("arbitrary", "arbitrary"), vmem_limit_bytes=VMEM_LIMIT_BYTES),
        name="kv_proj",
    )(mem, mem_norm.reshape(depth, 1, D_MODEL), w_kv, *[c[0] for c in casts])
    return outs[0], outs[1], outs[2:]


def _ffn_kernel(x_ref, g_ref, wg_ref, wu_ref, wd_ref, fin_ref, *refs, final_norm, n_cast):
    o_ref, (h_ref, act_ref) = refs[n_cast], refs[2 * n_cast + 1:]
    parts = _norm_parts(x_ref, g_ref[0], h_ref)
    for ci, (c0, cw) in enumerate(FF_CHUNKS):
        for rows in (parts if ci == 0 else (pl.ds(0, STEP_ROWS),)):
            gate = _dot(h_ref[rows, :], wg_ref[0, :, c0:c0 + cw])
            up = _dot(h_ref[rows, :], wu_ref[0, :, c0:c0 + cw])
            act_ref[rows, c0:c0 + cw] = (gate * jax.nn.sigmoid(gate) * up).astype(_BF16)
    if final_norm:
        half = STEP_ROWS // 2
        for rows in (pl.ds(0, half), pl.ds(half, half)):
            y = x_ref[rows, :] + 0.5 * _dot(act_ref[rows, :], wd_ref[0])
            o_ref[rows, :] = _rms_norm(y, fin_ref[...])
    else:
        o_ref[...] = x_ref[...] + 0.5 * _dot(act_ref[...], wd_ref[0])
    _cast_slices(refs[:n_cast], refs[n_cast + 1:2 * n_cast + 1])


def _ffn_call(x, g, g_layer, wg, wu, wd, fin, *, final_norm, casts=()):
    t = x.shape[0]
    n_steps = t // STEP_ROWS
    row = pl.BlockSpec((STEP_ROWS, D_MODEL), lambda i: (i, 0))
    jobs = [_cast_job(c, n_steps) for c in casts]
    outs = pl.pallas_call(
        functools.partial(_ffn_kernel, final_norm=final_norm, n_cast=len(casts)),
        grid=(n_steps,),
        in_specs=[
            row,
            _layer_resident(g.shape, g_layer),
            _layer_resident(wg.shape, 0),
            _layer_resident(wu.shape, 0),
            _layer_resident(wd.shape, 0),
            pl.BlockSpec(fin.shape, lambda i: (0, 0), pipeline_mode=pl.Buffered(1)),
        ] + [j[0] for j in jobs],
        out_specs=[row] + [j[1] for j in jobs],
        out_shape=[jax.ShapeDtypeStruct((t, D_MODEL), _F32)] + [j[2] for j in jobs],
        scratch_shapes=[pltpu.VMEM((STEP_ROWS, D_MODEL), _BF16),
                        pltpu.VMEM((STEP_ROWS, D_FF), _BF16)],
        compiler_params=pltpu.CompilerParams(
            dimension_semantics=("arbitrary",), vmem_limit_bytes=VMEM_LIMIT_BYTES),
        name="ffn",
    )(x, g, wg, wu, wd, fin, *[c[0] for c in casts])
    return outs[0], outs[1:]


def _cross_attention(q, kbd_ref, vbd_ref):
    s = _dot(q.astype(_BF16), kbd_ref[0, 0])
    es = []
    for hd in range(HEADS):
        sh = s[:, hd * N_MEM:(hd + 1) * N_MEM]
        es.append(jnp.exp(sh - jnp.max(sh, axis=-1, keepdims=True)).astype(_BF16))
    att = []
    for pr in range(HEADS // 2):
        pv = _dot(jnp.concatenate(es[2 * pr:2 * pr + 2], axis=1),
                  vbd_ref[0, 0, pl.ds(pr * 2 * N_MEM, 2 * N_MEM), :])
        att.append(pv[:, :PAIR_W] / pv[:, PAIR_W:])
    return jnp.concatenate(att, axis=1)


def _carry_in(carry_ref, step_in_batch):
    @pl.when(pl.program_id(0) == 0)
    def _():
        carry_ref[...] = jnp.zeros_like(carry_ref)

    return jnp.where(step_in_batch == 0, 0.0, carry_ref[...])


def _conv_mixer_kernel(x_ref, g_ref, win_ref, cw_ref, kbd_ref, vbd_ref, wout_ref, *refs,
                       steps_per_batch, n_cast):
    o_ref, (h_ref, carry_ref) = refs[n_cast], refs[2 * n_cast + 1:]
    carry = _carry_in(carry_ref, pl.program_id(0) % steps_per_batch)
    _cast_slices(refs[:n_cast], refs[n_cast + 1:2 * n_cast + 1])
    _norm_parts(x_ref, g_ref[0], h_ref)
    q = _dot(h_ref[...], win_ref[0, :, 3 * MW:])
    gcv = _dot(h_ref[...], win_ref[0, :, MW:3 * MW])
    cv = gcv[:, :MW] * gcv[:, MW:]
    gate_b = _dot(h_ref[...], win_ref[0, :, :MW])
    row = lax.broadcasted_iota(jnp.int32, (STEP_ROWS, 1), 0)
    prev1 = carry[CARRY_ROWS - 1:CARRY_ROWS, :]
    prev2 = carry[CARRY_ROWS - 2:CARRY_ROWS - 1, :]
    cv1 = jnp.where(row == 0, prev1, pltpu.roll(cv, 1, axis=0))
    cv2 = jnp.where(row == 0, prev2, jnp.where(row == 1, prev1, pltpu.roll(cv, 2, axis=0)))
    carry_ref[...] = cv[STEP_ROWS - CARRY_ROWS:, :]
    mix = gate_b * (cw_ref[0, 0:1, :] * cv2 + cw_ref[0, 1:2, :] * cv1 + cw_ref[0, 2:3, :] * cv)
    att = _cross_attention(q, kbd_ref, vbd_ref)
    cat = jnp.concatenate([mix.astype(_BF16), att.astype(_BF16)], axis=1)
    o_ref[...] = x_ref[...] + _dot(cat, wout_ref[0])


def _pool_mixer_kernel(x_ref, g_ref, win_ref, wgrp_ref, sc_ref, kbd_ref, vbd_ref, wout_ref,
                       *refs, steps_per_batch, n_cast):
    o_ref, (h_ref, carry_ref) = refs[n_cast], refs[2 * n_cast + 1:]
    step_in_batch = pl.program_id(0) % steps_per_batch
    carry = _carry_in(carry_ref, step_in_batch)
    _cast_slices(refs[:n_cast], refs[n_cast + 1:2 * n_cast + 1])
    _norm_parts(x_ref, g_ref[0], h_ref)
    z = _dot(h_ref[...], win_ref[0])
    p = z[:, :MW]
    q = z[:, MW:]
    ext = jnp.concatenate([carry, p], axis=0)
    carry_ref[...] = p[STEP_ROWS - CARRY_ROWS:, :]
    lane_group = lax.broadcasted_iota(jnp.int32, (1, MW), 1) // POOL_GROUP_DIM
    inv_win = jnp.zeros((1, MW), _F32)
    levels = []
    acc, base, span = ext, 0, 1
    for gi, win in enumerate(POOL_WINDOWS):
        first = gi * POOL_GROUP_DIM // LANES * LANES
        acc, base = acc[:, first - base:], first
        while span < win:
            acc = acc + pltpu.roll(acc, span, axis=0)
            span *= 2
        levels.append((acc, base))
        inv_win = jnp.where(lane_group == gi, 1.0 / win, inv_win)
    cols = []
    for c0 in range(0, MW, LANES):
        col = None
        for gi in range(c0 // POOL_GROUP_DIM, (c0 + LANES - 1) // POOL_GROUP_DIM + 1):
            sums, first = levels[gi]
            part = sums[CARRY_ROWS:, c0 - first:c0 - first + LANES]
            col = part if col is None else jnp.where(
                lane_group[:, c0:c0 + LANES] == gi, part, col)
        cols.append(col)
    wsum = jnp.concatenate(cols, axis=1)
    pos = step_in_batch * STEP_ROWS + lax.broadcasted_iota(jnp.int32, (STEP_ROWS, 1), 0)
    inv_count = jnp.maximum(1.0 / (pos.astype(_F32) + 1.0), inv_win)
    d = (wsum * inv_count - p).astype(_BF16)
    mix_cols = []
    for c0 in range(0, MW, MXU_TILE):
        first_group, last_group = c0 // POOL_GROUP_DIM, (c0 + MXU_TILE - 1) // POOL_GROUP_DIM
        k0 = first_group * POOL_GROUP_DIM // LANES * LANES
        k1 = min(MW, -(-(last_group + 1) * POOL_GROUP_DIM // LANES) * LANES)
        mix_cols.append(_dot(d[:, k0:k1], wgrp_ref[0, k0:k1, c0:c0 + MXU_TILE]))
    mix = jnp.concatenate(mix_cols, axis=1) * sc_ref[0]
    att = _cross_attention(q, kbd_ref, vbd_ref)
    cat = jnp.concatenate([mix.astype(_BF16), att.astype(_BF16)], axis=1)
    o_ref[...] = x_ref[...] + _dot(cat, wout_ref[0])


def _mixer_call(kernel_fn, name, x, layer, kbd, vbd, weights_before, w_out, seq, casts):
    t = x.shape[0]
    n_steps = t // STEP_ROWS
    steps_per_batch = seq // STEP_ROWS
    row = pl.BlockSpec((STEP_ROWS, D_MODEL), lambda i: (i, 0))
    kbd_spec = pl.BlockSpec((1, 1, XW, HEADS * N_MEM),
                            lambda i: (layer, i // steps_per_batch, 0, 0))
    vbd_spec = pl.BlockSpec((1, 1, HEADS * N_MEM, 2 * PAIR_W),
                            lambda i: (layer, i // steps_per_batch, 0, 0))
    jobs = [_cast_job(c, n_steps) for c in casts]
    outs = pl.pallas_call(
        functools.partial(kernel_fn, steps_per_batch=steps_per_batch, n_cast=len(casts)),
        grid=(n_steps,),
        in_specs=[row] + [_layer_resident(w.shape, li) for w, li in weights_before]
        + [kbd_spec, vbd_spec, _layer_resident(w_out.shape, 0)] + [j[0] for j in jobs],
        out_specs=[row] + [j[1] for j in jobs],
        out_shape=[jax.ShapeDtypeStruct((t, D_MODEL), _F32)] + [j[2] for j in jobs],
        scratch_shapes=[pltpu.VMEM((STEP_ROWS, D_MODEL), _BF16),
                        pltpu.VMEM((CARRY_ROWS, MW), _F32)],
        compiler_params=pltpu.CompilerParams(
            dimension_semantics=("arbitrary",), vmem_limit_bytes=VMEM_LIMIT_BYTES),
        name=name,
    )(x, *[w for w, _ in weights_before], kbd, vbd, w_out, *[c[0] for c in casts])
    return outs[0], outs[1:]


def _block_diag(w_group):
    _, g, d, _ = w_group.shape
    rows = [jnp.pad(w_group[:, gi], ((0, 0), (0, 0), (gi * d, (g - 1 - gi) * d)))
            for gi in range(g)]
    return jnp.concatenate(rows, axis=1)


def kernel(x, mem, ffn1_norm, ffn1_w_gu, ffn1_w_down, mix_norm, mem_norm, w_kv, w_out,
           conv_w_in, conv_w, pool_w_in, pool_w_group, pool_scale,
           ffn2_norm, ffn2_w_gu, ffn2_w_down, final_norm):
    batch, seq, _ = x.shape
    depth = ffn1_norm.shape[0]
    assert seq % STEP_ROWS == 0 and STEP_ROWS >= CARRY_ROWS >= max(POOL_WINDOWS) - 1
    assert conv_w.shape[1] == CONV_WIDTH == 3

    vec = lambda v: v.reshape(v.shape[0], 1, v.shape[-1])
    fin = final_norm.reshape(1, D_MODEL)
    g1, g2, mix_g = vec(ffn1_norm), vec(ffn2_norm), vec(mix_norm)
    pool_grp, pool_sc = _block_diag(pool_w_group).astype(_BF16), vec(pool_scale)

    kbd, vbd, ffn_w = _kv_call(mem, mem_norm, w_kv,
                               _ffn_weight_casts(ffn1_w_gu, ffn1_w_down, 0))
    mixer_casts = []
    ia = ib = 0
    for i in range(depth):
        mixer_casts += [(w_out, i), (conv_w_in, ia) if i % 2 == 0 else (pool_w_in, ib)]
        ia, ib = ia + (i % 2 == 0), ib + (i % 2 == 1)

    xt = x.reshape(batch * seq, D_MODEL)
    mixer_w = None
    ia = ib = 0
    for i in range(depth):
        xt, cast = _ffn_call(xt, g1, i, *ffn_w, fin, final_norm=False,
                             casts=mixer_casts if i == 0 else ())
        mixer_w = cast if i == 0 else mixer_w
        w_out_i, w_in_i = mixer_w[2 * i], mixer_w[2 * i + 1]
        casts = _ffn_weight_casts(ffn2_w_gu, ffn2_w_down, i)
        if i + 1 < depth:
            casts += _ffn_weight_casts(ffn1_w_gu, ffn1_w_down, i + 1)
        if i % 2 == 0:
            xt, cast = _mixer_call(_conv_mixer_kernel, "conv_mixer", xt, i, kbd, vbd,
                                   [(mix_g, i), (w_in_i, 0), (conv_w, ia)], w_out_i, seq,
                                   casts)
            ia += 1
        else:
            xt, cast = _mixer_call(_pool_mixer_kernel, "pool_mixer", xt, i, kbd, vbd,
                                   [(mix_g, i), (w_in_i, 0), (pool_grp, ib), (pool_sc, ib)],
                                   w_out_i, seq, casts)
            ib += 1
        xt, _ = _ffn_call(xt, g2, i, *cast[:3], fin, final_norm=(i == depth - 1))
        if i + 1 < depth:
            ffn_w = cast[3:6]
    return xt.reshape(batch, seq, D_MODEL)
```

```python
import functools

import jax
import jax.numpy as jnp
from jax import lax
from jax.experimental import pallas as pl
from jax.experimental.pallas import tpu as pltpu

D_MODEL = 1024
N_MEM = 256
HEADS = 4
HEAD_DIM = 64
XW = HEADS * HEAD_DIM
PAIR_W = 2 * HEAD_DIM
MW = D_MODEL - XW
CONV_WIDTH = 3
POOL_WINDOWS = (2, 4, 8, 16)
POOL_GROUP_DIM = MW // len(POOL_WINDOWS)
D_FF = 2816
RMS_EPS = 1e-6

STEP_ROWS = 1024
FF_CHUNKS = ((0, 512), (512, 512), (1024, 512), (1536, 512), (2048, 512), (2560, 256))
CARRY_ROWS = 16
BF16_SUBLANES = 16
LANES = 128
MXU_TILE = 256
VMEM_LIMIT_BYTES = 56 * 1024 * 1024

_BF16 = jnp.bfloat16
_F32 = jnp.float32


def _rms_norm(x, g):
    y = x * lax.rsqrt(jnp.mean(x * x, axis=-1, keepdims=True) + RMS_EPS)
    return y * g


def _dot(a, b):
    return jnp.dot(a, b, preferred_element_type=_F32)


def _layer_resident(shape, layer):
    nd = len(shape) - 1
    return pl.BlockSpec((1,) + tuple(shape[1:]), lambda *_: (layer,) + (0,) * nd,
                        pipeline_mode=pl.Buffered(1))


def _norm_parts(x_ref, g, h_ref):
    quarter = STEP_ROWS // 4
    parts = (pl.ds(0, quarter), pl.ds(quarter, quarter), pl.ds(2 * quarter, 2 * quarter))
    for rows in parts:
        h_ref[rows, :] = _rms_norm(x_ref[rows, :], g).astype(_BF16)
    return parts


def _cast_job(cast, n_steps, step_of=lambda i: i):
    w, layer, (col, width) = cast if len(cast) == 3 else (*cast, (0, cast[0].shape[2]))
    r = w.shape[1]
    n_blocks = n_steps
    while r % n_blocks or (r // n_blocks) % BF16_SUBLANES:
        n_blocks //= 2
    block = (1, r // n_blocks, width)
    src = pl.BlockSpec(block, lambda *i: (layer, step_of(*i) * n_blocks // n_steps, col))
    dst = pl.BlockSpec(block, lambda *i: (0, step_of(*i) * n_blocks // n_steps, 0))
    return src, dst, jax.ShapeDtypeStruct((1, r, width), _BF16)


def _ffn_weight_casts(w_gu, w_down, layer):
    return [(w_gu, layer, (0, D_FF)), (w_gu, layer, (1, D_FF)), (w_down, layer)]


def _cast_slices(srcs, dsts):
    for src, dst in zip(srcs, dsts):
        dst[...] = src[...].astype(_BF16)


def _kv_kernel(mem_ref, g_ref, wkv_ref, *refs, n_cast):
    kbd_ref, vbd_ref = refs[n_cast:n_cast + 2]
    mem_h = _rms_norm(mem_ref[0], g_ref[0])
    kv = _dot(mem_h, wkv_ref[0])
    k = kv[:, :XW] * (HEAD_DIM ** -0.5)
    v = kv[:, XW:]
    kt = jnp.concatenate([k.T] * HEADS, axis=1)
    r = lax.broadcasted_iota(jnp.int32, kt.shape, 0) // HEAD_DIM
    c = lax.broadcasted_iota(jnp.int32, kt.shape, 1) // N_MEM
    kbd_ref[0, 0] = jnp.where(r == c, kt, 0.0).astype(_BF16)
    for pr in range(HEADS // 2):
        vp = v[:, pr * PAIR_W:(pr + 1) * PAIR_W]
        vp2 = jnp.concatenate([vp, vp], axis=0)
        r = lax.broadcasted_iota(jnp.int32, vp2.shape, 0) // N_MEM
        c = lax.broadcasted_iota(jnp.int32, vp2.shape, 1) // HEAD_DIM
        own = r == c
        rows = pl.ds(pr * 2 * N_MEM, 2 * N_MEM)
        vbd_ref[0, 0, rows, :PAIR_W] = jnp.where(own, vp2, 0.0).astype(_BF16)
        vbd_ref[0, 0, rows, PAIR_W:] = jnp.where(own, 1.0, 0.0).astype(_BF16)
    _cast_slices(refs[:n_cast], refs[n_cast + 2:])


def _kv_call(mem, mem_norm, w_kv, casts):
    depth, batch = mem_norm.shape[0], mem.shape[0]
    jobs = [_cast_job(c, depth * batch, lambda l, b: l * batch + b) for c in casts]
    outs = pl.pallas_call(
        functools.partial(_kv_kernel, n_cast=len(casts)),
        grid=(depth, batch),
        in_specs=[
            pl.BlockSpec((1, N_MEM, D_MODEL), lambda l, b: (b, 0, 0)),
            pl.BlockSpec((1, 1, D_MODEL), lambda l, b: (l, 0, 0)),
            pl.BlockSpec((1, D_MODEL, 2 * XW), lambda l, b: (l, 0, 0)),
        ] + [j[0] for j in jobs],
        out_specs=[
            pl.BlockSpec((1, 1, XW, HEADS * N_MEM), lambda l, b: (l, b, 0, 0)),
            pl.BlockSpec((1, 1, HEADS * N_MEM, 2 * PAIR_W), lambda l, b: (l, b, 0, 0)),
        ] + [j[1] for j in jobs],
        out_shape=[
            jax.ShapeDtypeStruct((depth, batch, XW, HEADS * N_MEM), _BF16),
            jax.ShapeDtypeStruct((depth, batch, HEADS * N_MEM, 2 * PAIR_W), _BF16),
        ] + [j[2] for j in jobs],
        compiler_params=pltpu.CompilerParams(
            dimension_semantics=("arbitrary", "arbitrary"), vmem_limit_bytes=VMEM_LIMIT_BYTES),
        name="kv_proj",
    )(mem, mem_norm.reshape(depth, 1, D_MODEL), w_kv, *[c[0] for c in casts])
    return outs[0], outs[1], outs[2:]


def _ffn_kernel(x_ref, g_ref, wg_ref, wu_ref, wd_ref, fin_ref, *refs, final_norm, n_cast):
    o_ref, (h_ref, act_ref) = refs[n_cast], refs[2 * n_cast + 1:]
    parts = _norm_parts(x_ref, g_ref[0], h_ref)
    for ci, (c0, cw) in enumerate(FF_CHUNKS):
        for rows in (parts if ci == 0 else (pl.ds(0, STEP_ROWS),)):
            gate = _dot(h_ref[rows, :], wg_ref[0, :, c0:c0 + cw])
            up = _dot(h_ref[rows, :], wu_ref[0, :, c0:c0 + cw])
            silu = gate * (0.5 * jnp.tanh(0.5 * gate) + 0.5)
            act_ref[rows, c0:c0 + cw] = (silu * up).astype(_BF16)
    if final_norm:
        half = STEP_ROWS // 2
        for rows in (pl.ds(0, half), pl.ds(half, half)):
            y = x_ref[rows, :] + 0.5 * _dot(act_ref[rows, :], wd_ref[0])
            o_ref[rows, :] = _rms_norm(y, fin_ref[...])
    else:
        o_ref[...] = x_ref[...] + 0.5 * _dot(act_ref[...], wd_ref[0])
    _cast_slices(refs[:n_cast], refs[n_cast + 1:2 * n_cast + 1])


def _ffn_call(x, g, g_layer, wg, wu, wd, fin, *, final_norm, casts=()):
    t = x.shape[0]
    n_steps = t // STEP_ROWS
    row = pl.BlockSpec((STEP_ROWS, D_MODEL), lambda i: (i, 0))
    jobs = [_cast_job(c, n_steps) for c in casts]
    outs = pl.pallas_call(
        functools.partial(_ffn_kernel, final_norm=final_norm, n_cast=len(casts)),
        grid=(n_steps,),
        in_specs=[
            row,
            _layer_resident(g.shape, g_layer),
            _layer_resident(wg.shape, 0),
            _layer_resident(wu.shape, 0),
            _layer_resident(wd.shape, 0),
            pl.BlockSpec(fin.shape, lambda i: (0, 0), pipeline_mode=pl.Buffered(1)),
        ] + [j[0] for j in jobs],
        out_specs=[row] + [j[1] for j in jobs],
        out_shape=[jax.ShapeDtypeStruct((t, D_MODEL), _F32)] + [j[2] for j in jobs],
        scratch_shapes=[pltpu.VMEM((STEP_ROWS, D_MODEL), _BF16),
                        pltpu.VMEM((STEP_ROWS, D_FF), _BF16)],
        compiler_params=pltpu.CompilerParams(
            dimension_semantics=("arbitrary",), vmem_limit_bytes=VMEM_LIMIT_BYTES),
        name="ffn",
    )(x, g, wg, wu, wd, fin, *[c[0] for c in casts])
    return outs[0], outs[1:]


def _cross_attention(q, kbd_ref, vbd_ref):
    s = _dot(q.astype(_BF16), kbd_ref[0, 0])
    es = []
    for hd in range(HEADS):
        sh = s[:, hd * N_MEM:(hd + 1) * N_MEM]
        es.append(jnp.exp(sh - jnp.max(sh, axis=-1, keepdims=True)).astype(_BF16))
    att = []
    for pr in range(HEADS // 2):
        pv = _dot(jnp.concatenate(es[2 * pr:2 * pr + 2], axis=1),
                  vbd_ref[0, 0, pl.ds(pr * 2 * N_MEM, 2 * N_MEM), :])
        att.append(pv[:, :PAIR_W] / pv[:, PAIR_W:])
    return jnp.concatenate(att, axis=1)


def _carry_in(carry_ref, step_in_batch):
    @pl.when(pl.program_id(0) == 0)
    def _():
        carry_ref[...] = jnp.zeros_like(carry_ref)

    return jnp.where(step_in_batch == 0, 0.0, carry_ref[...])


def _conv_mixer_kernel(x_ref, g_ref, win_ref, cw_ref, kbd_ref, vbd_ref, wout_ref, *refs,
                       steps_per_batch, n_cast):
    o_ref, (h_ref, carry_ref) = refs[n_cast], refs[2 * n_cast + 1:]
    carry = _carry_in(carry_ref, pl.program_id(0) % steps_per_batch)
    _cast_slices(refs[:n_cast], refs[n_cast + 1:2 * n_cast + 1])
    _norm_parts(x_ref, g_ref[0], h_ref)
    q = _dot(h_ref[...], win_ref[0, :, 3 * MW:])
    gcv = _dot(h_ref[...], win_ref[0, :, MW:3 * MW])
    cv = gcv[:, :MW] * gcv[:, MW:]
    gate_b = _dot(h_ref[...], win_ref[0, :, :MW])
    row = lax.broadcasted_iota(jnp.int32, (STEP_ROWS, 1), 0)
    prev1 = carry[CARRY_ROWS - 1:CARRY_ROWS, :]
    prev2 = carry[CARRY_ROWS - 2:CARRY_ROWS - 1, :]
    cv1 = jnp.where(row == 0, prev1, pltpu.roll(cv, 1, axis=0))
    cv2 = jnp.where(row == 0, prev2, jnp.where(row == 1, prev1, pltpu.roll(cv, 2, axis=0)))
    carry_ref[...] = cv[STEP_ROWS - CARRY_ROWS:, :]
    mix = gate_b * (cw_ref[0, 0:1, :] * cv2 + cw_ref[0, 1:2, :] * cv1 + cw_ref[0, 2:3, :] * cv)
    att = _cross_attention(q, kbd_ref, vbd_ref)
    cat = jnp.concatenate([mix.astype(_BF16), att.astype(_BF16)], axis=1)
    o_ref[...] = x_ref[...] + _dot(cat, wout_ref[0])


def _pool_mixer_kernel(x_ref, g_ref, win_ref, wgrp_ref, sc_ref, kbd_ref, vbd_ref, wout_ref,
                       *refs, steps_per_batch, n_cast):
    o_ref, (h_ref, carry_ref) = refs[n_cast], refs[2 * n_cast + 1:]
    step_in_batch = pl.program_id(0) % steps_per_batch
    carry = _carry_in(carry_ref, step_in_batch)
    _cast_slices(refs[:n_cast], refs[n_cast + 1:2 * n_cast + 1])
    _norm_parts(x_ref, g_ref[0], h_ref)
    z = _dot(h_ref[...], win_ref[0])
    p = z[:, :MW]
    q = z[:, MW:]
    ext = jnp.concatenate([carry, p], axis=0)
    carry_ref[...] = p[STEP_ROWS - CARRY_ROWS:, :]
    lane_group = lax.broadcasted_iota(jnp.int32, (1, MW), 1) // POOL_GROUP_DIM
    inv_win = jnp.zeros((1, MW), _F32)
    levels = []
    acc, base, span = ext, 0, 1
    for gi, win in enumerate(POOL_WINDOWS):
        first = gi * POOL_GROUP_DIM // LANES * LANES
        acc, base = acc[:, first - base:], first
        while span < win:
            acc = acc + pltpu.roll(acc, span, axis=0)
            span *= 2
        levels.append((acc, base))
        inv_win = jnp.where(lane_group == gi, 1.0 / win, inv_win)
    cols = []
    for c0 in range(0, MW, LANES):
        col = None
        for gi in range(c0 // POOL_GROUP_DIM, (c0 + LANES - 1) // POOL_GROUP_DIM + 1):
            sums, first = levels[gi]
            part = sums[CARRY_ROWS:, c0 - first:c0 - first + LANES]
            col = part if col is None else jnp.where(
                lane_group[:, c0:c0 + LANES] == gi, part, col)
        cols.append(col)
    wsum = jnp.concatenate(cols, axis=1)
    pos = step_in_batch * STEP_ROWS + lax.broadcasted_iota(jnp.int32, (STEP_ROWS, 1), 0)
    inv_count = jnp.maximum(1.0 / (pos.astype(_F32) + 1.0), inv_win)
    d = (wsum * inv_count - p).astype(_BF16)
    mix_cols = []
    for c0 in range(0, MW, MXU_TILE):
        first_group, last_group = c0 // POOL_GROUP_DIM, (c0 + MXU_TILE - 1) // POOL_GROUP_DIM
        k0 = first_group * POOL_GROUP_DIM // LANES * LANES
        k1 = min(MW, -(-(last_group + 1) * POOL_GROUP_DIM // LANES) * LANES)
        mix_cols.append(_dot(d[:, k0:k1], wgrp_ref[0, k0:k1, c0:c0 + MXU_TILE]))
    mix = jnp.concatenate(mix_cols, axis=1) * sc_ref[0]
    att = _cross_attention(q, kbd_ref, vbd_ref)
    cat = jnp.concatenate([mix.astype(_BF16), att.astype(_BF16)], axis=1)
    o_ref[...] = x_ref[...] + _dot(cat, wout_ref[0])


def _mixer_call(kernel_fn, name, x, layer, kbd, vbd, weights_before, w_out, seq, casts):
    t = x.shape[0]
    n_steps = t // STEP_ROWS
    steps_per_batch = seq // STEP_ROWS
    row = pl.BlockSpec((STEP_ROWS, D_MODEL), lambda i: (i, 0))
    kbd_spec = pl.BlockSpec((1, 1, XW, HEADS * N_MEM),
                            lambda i: (layer, i // steps_per_batch, 0, 0))
    vbd_spec = pl.BlockSpec((1, 1, HEADS * N_MEM, 2 * PAIR_W),
                            lambda i: (layer, i // steps_per_batch, 0, 0))
    jobs = [_cast_job(c, n_steps) for c in casts]
    outs = pl.pallas_call(
        functools.partial(kernel_fn, steps_per_batch=steps_per_batch, n_cast=len(casts)),
        grid=(n_steps,),
        in_specs=[row] + [_layer_resident(w.shape, li) for w, li in weights_before]
        + [kbd_spec, vbd_spec, _layer_resident(w_out.shape, 0)] + [j[0] for j in jobs],
        out_specs=[row] + [j[1] for j in jobs],
        out_shape=[jax.ShapeDtypeStruct((t, D_MODEL), _F32)] + [j[2] for j in jobs],
        scratch_shapes=[pltpu.VMEM((STEP_ROWS, D_MODEL), _BF16),
                        pltpu.VMEM((CARRY_ROWS, MW), _F32)],
        compiler_params=pltpu.CompilerParams(
            dimension_semantics=("arbitrary",), vmem_limit_bytes=VMEM_LIMIT_BYTES),
        name=name,
    )(x, *[w for w, _ in weights_before], kbd, vbd, w_out, *[c[0] for c in casts])
    return outs[0], outs[1:]


def _block_diag(w_group):
    _, g, d, _ = w_group.shape
    rows = [jnp.pad(w_group[:, gi], ((0, 0), (0, 0), (gi * d, (g - 1 - gi) * d)))
            for gi in range(g)]
    return jnp.concatenate(rows, axis=1)


def kernel(x, mem, ffn1_norm, ffn1_w_gu, ffn1_w_down, mix_norm, mem_norm, w_kv, w_out,
           conv_w_in, conv_w, pool_w_in, pool_w_group, pool_scale,
           ffn2_norm, ffn2_w_gu, ffn2_w_down, final_norm):
    batch, seq, _ = x.shape
    depth = ffn1_norm.shape[0]
    assert seq % STEP_ROWS == 0 and STEP_ROWS >= CARRY_ROWS >= max(POOL_WINDOWS) - 1
    assert conv_w.shape[1] == CONV_WIDTH == 3

    vec = lambda v: v.reshape(v.shape[0], 1, v.shape[-1])
    fin = final_norm.reshape(1, D_MODEL)
    g1, g2, mix_g = vec(ffn1_norm), vec(ffn2_norm), vec(mix_norm)
    pool_grp, pool_sc = _block_diag(pool_w_group).astype(_BF16), vec(pool_scale)

    kbd, vbd, ffn_w = _kv_call(mem, mem_norm, w_kv,
                               _ffn_weight_casts(ffn1_w_gu, ffn1_w_down, 0))
    mixer_casts = []
    ia = ib = 0
    for i in range(depth):
        mixer_casts += [(w_out, i), (conv_w_in, ia) if i % 2 == 0 else (pool_w_in, ib)]
        ia, ib = ia + (i % 2 == 0), ib + (i % 2 == 1)

    xt = x.reshape(batch * seq, D_MODEL)
    mixer_w = None
    ia = ib = 0
    for i in range(depth):
        xt, cast = _ffn_call(xt, g1, i, *ffn_w, fin, final_norm=False,
                             casts=mixer_casts if i == 0 else ())
        mixer_w = cast if i == 0 else mixer_w
        w_out_i, w_in_i = mixer_w[2 * i], mixer_w[2 * i + 1]
        casts = _ffn_weight_casts(ffn2_w_gu, ffn2_w_down, i)
        if i + 1 < depth:
            casts += _ffn_weight_casts(ffn1_w_gu, ffn1_w_down, i + 1)
        if i % 2 == 0:
            xt, cast = _mixer_call(_conv_mixer_kernel, "conv_mixer", xt, i, kbd, vbd,
                                   [(mix_g, i), (w_in_i, 0), (conv_w, ia)], w_out_i, seq,
                                   casts)
            ia += 1
        else:
            xt, cast = _mixer_call(_pool_mixer_kernel, "pool_mixer", xt, i, kbd, vbd,
                                   [(mix_g, i), (w_in_i, 0), (pool_grp, ib), (pool_sc, ib)],
                                   w_out_i, seq, casts)
            ib += 1
        xt, _ = _ffn_call(xt, g2, i, *cast[:3], fin, final_norm=(i == depth - 1))
        if i + 1 < depth:
            ffn_w = cast[3:6]
    return xt.reshape(batch, seq, D_MODEL)
```
